```python
import jax, jax.numpy as jnp
from jax import lax
import numpy as np

D_MODEL = 2048
BATCH = 4
SEQ = 4096
DEPTH = 1
DEC_BATCH = 16
DEC_SEQ = 16
PAST_LEN = 4096

CHUNK = 64
MIX_DIM = D_MODEL
CONV_DIM = MIX_DIM // 2
HGRN_DIM = MIX_DIM - CONV_DIM
HGRN_HEADS = 8
HEAD_K = HGRN_DIM // HGRN_HEADS
HEAD_V = HGRN_DIM // HGRN_HEADS
CONV_W = 31
D_FF = 4 * D_MODEL
RBLK = CHUNK // 4
IN_COLS = 2 * CONV_DIM + 4 * HGRN_DIM
EPS = 1e-6

kernel_name = "hymba_conformer_hgrn2_stream_step"


def rms_norm(x, g):
    xf = x.astype(jnp.float32)
    y = xf * lax.rsqrt(jnp.mean(xf * xf, axis=-1, keepdims=True) + EPS)
    return (y * g.astype(jnp.float32)).astype(x.dtype)


def conformer_conv(u_in, buf, w_dw, b_dw, ln_g, ln_b):
    a, gate = jnp.split(u_in, 2, axis=-1)
    u = a * jax.nn.sigmoid(gate)
    full = jnp.concatenate([buf.astype(u.dtype), u], axis=1)
    c = lax.conv_general_dilated(
        full, w_dw[:, None, :].astype(u.dtype), (1,), 'VALID',
        dimension_numbers=('NWC', 'WIO', 'NWC'),
        feature_group_count=CONV_DIM) + b_dw.astype(u.dtype)
    cf = c.astype(jnp.float32)
    mu = jnp.mean(cf, axis=-1, keepdims=True)
    var = jnp.mean(jnp.square(cf - mu), axis=-1, keepdims=True)
    cn = (cf - mu) * lax.rsqrt(var + EPS) * ln_g.astype(jnp.float32) + ln_b.astype(jnp.float32)
    out = jax.nn.silu(cn).astype(u.dtype)
    return out, full[:, -(CONV_W - 1):]


def hgrn2(q_pre, f_pre, i_in, g_pre, lb, S0, norm_g):
    B, T, _ = q_pre.shape
    f32 = jnp.float32
    lbf = lb.astype(f32)
    q = jax.nn.silu(q_pre.astype(f32)).reshape(B, T, HGRN_HEADS, HEAD_K)
    f = lbf + (1.0 - lbf) * jax.nn.sigmoid(f_pre.astype(f32))
    logf = jnp.log(f).reshape(B, T, HGRN_HEADS, HEAD_K)
    k = (1.0 - f).reshape(B, T, HGRN_HEADS, HEAD_K)
    v = i_in.astype(f32).reshape(B, T, HGRN_HEADS, HEAD_V)
    pad = (-T) % RBLK
    n_blk = (T + pad) // RBLK

    def blocks(z):
        z = jnp.pad(z, ((0, 0), (0, pad), (0, 0), (0, 0)))
        return z.reshape(B, n_blk, RBLK, HGRN_HEADS, z.shape[-1]).transpose(1, 0, 3, 2, 4)

    qb, kb, vb, lfb = blocks(q), blocks(k), blocks(v), blocks(logf)
    bcum = jnp.cumsum(lfb, axis=3)
    bend = bcum[:, :, :, -1:, :]
    qd = qb * jnp.exp(bcum)
    kd = kb * jnp.exp(-bcum)
    kend = kb * jnp.exp(bend - bcum)
    gend = jnp.exp(bend[:, :, :, 0, :])
    mask = jnp.tril(jnp.ones((RBLK, RBLK), dtype=bool))
    att = jnp.where(mask, jnp.einsum('nbhld,nbhmd->nbhlm', qd, kd), 0.0)
    o_intra = jnp.einsum('nbhlm,nbhme->nbhle', att, vb)

    def step(S, xs):
        qd_j, kend_j, v_j, g_j = xs
        o_j = jnp.einsum('bhld,bhde->bhle', qd_j, S)
        S = g_j[..., None] * S + jnp.einsum('bhld,bhle->bhde', kend_j, v_j)
        return S, o_j

    S_T, o_inter = lax.scan(step, S0.astype(f32), (qd, kend, vb, gend))
    o = (o_intra + o_inter).transpose(1, 0, 3, 2, 4).reshape(B, n_blk * RBLK, HGRN_HEADS, HEAD_V)[:, :T]
    o = o * lax.rsqrt(jnp.mean(o * o, axis=-1, keepdims=True) + EPS)
    o = o.reshape(B, T, HGRN_DIM) * norm_g.astype(f32) * jax.nn.silu(g_pre.astype(f32))
    return o.astype(q_pre.dtype), S_T.astype(S0.dtype)


def layer(x, conv_buf, S0, lb, g_mix, w_in, w_dw, b_dw, ln_g, ln_b, hgrn_g, w_out, g_mlp, w_up, w_down):
    n = rms_norm(x, g_mix)
    proj = n @ w_in
    conv_in = proj[..., :2 * CONV_DIM]
    q_pre, f_pre, i_in, g_pre = jnp.split(proj[..., 2 * CONV_DIM:], 4, axis=-1)
    a_out, new_buf = conformer_conv(conv_in, conv_buf, w_dw, b_dw, ln_g, ln_b)
    b_out, S_T = hgrn2(q_pre, f_pre, i_in, g_pre, lb, S0, hgrn_g)
    h = x + jnp.concatenate([a_out, b_out], axis=-1) @ w_out
    m = rms_norm(h, g_mlp)
    h = h + jnp.square(jax.nn.relu(m @ w_up)) @ w_down
    return h, new_buf, S_T


def setup_inputs(seed: int = 0) -> dict:
    key = jax.random.key(seed)
    ks = jax.random.split(key, 20)
    nrm = jax.random.normal
    f32 = jnp.float32
    return {
        "x_prompt": nrm(ks[0], (BATCH, SEQ, D_MODEL), f32),
        "x_sample": nrm(ks[1], (DEC_BATCH, DEC_SEQ, D_MODEL), f32),
        "state_conv": 0.5 * nrm(ks[2], (DEPTH, DEC_BATCH, CONV_W - 1, CONV_DIM), f32),
        "state_hgrn": 0.5 * nrm(ks[3], (DEPTH, DEC_BATCH, HGRN_HEADS, HEAD_K, HEAD_V), f32),
        "norm_mix_g": 1.0 + 0.02 * nrm(ks[4], (DEPTH, D_MODEL), f32),
        "w_in": nrm(ks[5], (DEPTH, D_MODEL, IN_COLS), f32) * D_MODEL ** -0.5,
        "w_dw": nrm(ks[6], (DEPTH, CONV_W, CONV_DIM), f32) * CONV_W ** -0.5,
        "b_dw": 0.02 * nrm(ks[7], (DEPTH, CONV_DIM), f32),
        "ln_conv_g": 1.0 + 0.02 * nrm(ks[8], (DEPTH, CONV_DIM), f32),
        "ln_conv_b": 0.02 * nrm(ks[9], (DEPTH, CONV_DIM), f32),
        "lb_logits": 0.1 * nrm(ks[10], (DEPTH + 1, HGRN_DIM), f32),
        "hgrn_norm_g": 1.0 + 0.02 * nrm(ks[11], (DEPTH, HGRN_DIM), f32),
        "w_out": nrm(ks[12], (DEPTH, MIX_DIM, D_MODEL), f32) * MIX_DIM ** -0.5,
        "norm_mlp_g": 1.0 + 0.02 * nrm(ks[13], (DEPTH, D_MODEL), f32),
        "w_up": nrm(ks[14], (DEPTH, D_MODEL, D_FF), f32) * D_MODEL ** -0.5,
        "w_down": nrm(ks[15], (DEPTH, D_FF, D_MODEL), f32) * D_FF ** -0.5,
        "norm_final_g": 1.0 + 0.02 * nrm(ks[16], (D_MODEL,), f32),
    }


def reference(x_prompt, x_sample, state_conv, state_hgrn, norm_mix_g, w_in, w_dw, b_dw, ln_conv_g, ln_conv_b,
              lb_logits, hgrn_norm_g, w_out, norm_mlp_g, w_up, w_down, norm_final_g):
    lb_all = jnp.cumsum(jax.nn.softmax(lb_logits.astype(jnp.float32), axis=0), axis=0)
    bp = x_prompt.shape[0]
    hp, hs = x_prompt, x_sample
    conv_p, hgrn_p, conv_s, hgrn_s = [], [], [], []
    for l in range(DEPTH):
        w = (lb_all[l], norm_mix_g[l], w_in[l], w_dw[l], b_dw[l], ln_conv_g[l], ln_conv_b[l],
             hgrn_norm_g[l], w_out[l], norm_mlp_g[l], w_up[l], w_down[l])
        zero_buf = jnp.zeros((bp, CONV_W - 1, CONV_DIM), hp.dtype)
        zero_S = jnp.zeros((bp, HGRN_HEADS, HEAD_K, HEAD_V), state_hgrn.dtype)
        hp, cbp, sp = layer(hp, zero_buf, zero_S, *w)
        hs, cbs, ss = layer(hs, state_conv[l], state_hgrn[l], *w)
        conv_p.append(cbp)
        hgrn_p.append(sp)
        conv_s.append(cbs)
        hgrn_s.append(ss)
    y_prompt = rms_norm(hp, norm_final_g)
    y_sample = rms_norm(hs, norm_final_g)
    return (y_prompt, y_sample, jnp.stack(conv_p), jnp.stack(hgrn_p), jnp.stack(conv_s), jnp.stack(hgrn_s))
```

```python
import functools

import jax
import jax.numpy as jnp
from jax import lax
from jax.experimental import pallas as pl
from jax.experimental.pallas import tpu as pltpu

EPS = 1e-6
HEAD_DIM = 128
CONV_HALO = 32
V7X_VMEM_LIMIT = 56 * 1024 * 1024

F32 = jnp.float32
BF16 = jnp.bfloat16


def _pick(n, candidates):
    for c in candidates:
        if n % c == 0:
            return c
    return n


def _params(*sem):
    return pltpu.CompilerParams(dimension_semantics=sem, vmem_limit_bytes=V7X_VMEM_LIMIT)


def _in_proj_kernel(x_ref, g_ref, w_ref, o_ref, n_ref):
    @pl.when(pl.program_id(1) == 0)
    def _():
        x = x_ref[...]
        ms = jnp.mean(x * x, axis=-1, keepdims=True)
        n_ref[...] = (x * lax.rsqrt(ms + EPS) * g_ref[...]).astype(BF16)

    o_ref[...] = jnp.dot(n_ref[...], w_ref[...], preferred_element_type=F32)


def _in_proj(x, g, w):
    n, d = x.shape
    cols = w.shape[1]
    tm = _pick(n, (1024, 512, 256))
    tn = _pick(cols, (1024, 512))
    return pl.pallas_call(
        _in_proj_kernel,
        grid=(n // tm, cols // tn),
        in_specs=[
            pl.BlockSpec((tm, d), lambda i, j: (i, 0)),
            pl.BlockSpec((1, d), lambda i, j: (0, 0)),
            pl.BlockSpec((d, tn), lambda i, j: (0, j)),
        ],
        out_specs=pl.BlockSpec((tm, tn), lambda i, j: (i, j)),
        out_shape=jax.ShapeDtypeStruct((n, cols), F32),
        scratch_shapes=[pltpu.VMEM((tm, d), BF16)],
        compiler_params=_params("parallel", "arbitrary"),
        name="in_proj",
    )(x, g, w)


def _conv_kernel(p_ref, st_ref, w_ref, b_ref, lg_ref, lb_ref, a_ref, ns_ref, ubuf, *, tt, rc, cdim, cw):
    t = pl.program_id(1)
    hist = cw - 1

    @pl.when(t == 0)
    def _():
        ubuf[0:CONV_HALO, :] = jnp.zeros((CONV_HALO, cdim), F32)
        ubuf[CONV_HALO - hist:CONV_HALO, :] = st_ref[0]

    @pl.when(t > 0)
    def _():
        ubuf[0:CONV_HALO, :] = ubuf[tt:tt + CONV_HALO, :]

    lanes = HEAD_DIM
    nlb = cdim // lanes

    def chunk(i, carry):
        base = pl.multiple_of(i * rc, rc)
        a = p_ref[0, pl.ds(base, rc), 0:cdim]
        gate = p_ref[0, pl.ds(base, rc), cdim:2 * cdim]
        ubuf[pl.ds(CONV_HALO + base, rc), :] = a * jax.nn.sigmoid(gate)
        pieces = []
        for lb in range(nlb):
            ls = slice(lb * lanes, (lb + 1) * lanes)
            acc = jnp.broadcast_to(b_ref[0:1, ls], (rc, lanes))
            win = ubuf[pl.ds(base, rc + CONV_HALO), ls]
            for res in range(8):
                offs = [o for o in range(CONV_HALO - hist, CONV_HALO + 1) if o % 8 == res]
                if not offs:
                    continue
                sh = pltpu.roll(win, rc + CONV_HALO - res, axis=0) if res else win
                for o in offs:
                    k = o - (CONV_HALO - hist)
                    acc = acc + sh[o - res:o - res + rc] * w_ref[k:k + 1, ls]
            pieces.append(acc)
        c = jnp.concatenate(pieces, axis=-1)
        mu = jnp.mean(c, axis=-1, keepdims=True)
        cc = c - mu
        var = jnp.mean(cc * cc, axis=-1, keepdims=True)
        cn = cc * lax.rsqrt(var + EPS) * lg_ref[...] + lb_ref[...]
        a_ref[0, pl.ds(base, rc), :] = (cn * jax.nn.sigmoid(cn)).astype(BF16)
        return carry

    lax.fori_loop(0, tt // rc, chunk, 0)

    @pl.when(t == pl.num_programs(1) - 1)
    def _():
        ns_ref[0] = ubuf[tt + CONV_HALO - hist:tt + CONV_HALO, :]


def _conv_group(p3, state, w_dw, b_dw, ln_g, ln_b):
    b, t, _ = p3.shape
    cw, cdim = w_dw.shape
    hist = cw - 1
    assert hist <= CONV_HALO
    tt = _pick(t, (256, 128, 64, 32, 16))
    rc = min(tt, 32)
    kern = functools.partial(_conv_kernel, tt=tt, rc=rc, cdim=cdim, cw=cw)
    return pl.pallas_call(
        kern,
        grid=(b, t // tt),
        in_specs=[
            pl.BlockSpec((1, tt, 2 * cdim), lambda i, j: (i, j, 0)),
            pl.BlockSpec((1, hist, cdim), lambda i, j: (i, 0, 0)),
            pl.BlockSpec((cw, cdim), lambda i, j: (0, 0)),
            pl.BlockSpec((1, cdim), lambda i, j: (0, 0)),
            pl.BlockSpec((1, cdim), lambda i, j: (0, 0)),
            pl.BlockSpec((1, cdim), lambda i, j: (0, 0)),
        ],
        out_specs=[
            pl.BlockSpec((1, tt, cdim), lambda i, j: (i, j, 0)),
            pl.BlockSpec((1, hist, cdim), lambda i, j: (i, 0, 0)),
        ],
        out_shape=[
            jax.ShapeDtypeStruct((b, t, cdim), BF16),
            jax.ShapeDtypeStruct((b, hist, cdim), F32),
        ],
        scratch_shapes=[pltpu.VMEM((CONV_HALO + max(tt, CONV_HALO), cdim), F32)],
        compiler_params=_params("parallel", "arbitrary"),
        name="conv_group",
    )(p3, state, w_dw, b_dw, ln_g, ln_b)


def _cumsum_rows(x, n):
    row = lax.broadcasted_iota(jnp.int32, x.shape, 0)
    s = 1
    while s < n:
        x = x + jnp.where(row >= s, pltpu.roll(x, s, axis=0), 0.0)
        s *= 2
    return x


def _hgrn_kernel(q_ref, f_ref, i_ref, g_ref, s0_ref, lbl_ref, ng_ref, o_ref, sn_ref, st_ref,
                 *, tt, ch, blk, hb, layer):
    t = pl.program_id(2)
    d = HEAD_DIM
    nb = ch // blk
    half = blk // 2

    @pl.when(t == 0)
    def _():
        for h in range(hb):
            st_ref[h] = s0_ref[0, h].T

    lg = lbl_ref[...]
    e = jnp.exp(lg - jnp.max(lg, axis=0, keepdims=True))
    lbv = jnp.sum(e[0:layer + 1], axis=0, keepdims=True) / jnp.sum(e, axis=0, keepdims=True)
    ngv = ng_ref[...]

    tri = (lax.broadcasted_iota(jnp.int32, (ch, ch), 0) >= lax.broadcasted_iota(jnp.int32, (ch, ch), 1))

    def chunk(c, carry):
        rows = pl.ds(pl.multiple_of(c * ch, ch), ch)
        qp = q_ref[0, rows, :]
        fp = f_ref[0, rows, :]
        v = i_ref[0, rows, :]
        gp = g_ref[0, rows, :]
        q = qp * jax.nn.sigmoid(qp)
        f = lbv + (1.0 - lbv) * jax.nn.sigmoid(fp)
        k = 1.0 - f
        cum = _cumsum_rows(jnp.log(f), ch)
        mids = [cum[j * blk + half - 1:j * blk + half, :] for j in range(nb)]
        last = cum[ch - 1:ch, :]
        mid_rows = jnp.concatenate([jnp.broadcast_to(m, (blk, m.shape[1])) for m in mids], axis=0)
        qd = q * jnp.exp(cum - mid_rows)
        kd = k * jnp.exp(mid_rows - cum)
        q_in = qd * jnp.exp(mid_rows)
        k_end = kd * jnp.exp(last - mid_rows)
        g_end = jnp.exp(last)
        gate = gp * jax.nn.sigmoid(gp)
        vb = v.astype(BF16)
        outs = []
        for h in range(hb):
            ls = slice(h * d, (h + 1) * d)
            qd_h, kd_h = qd[:, ls], kd[:, ls]
            qcols, kcols = [], []
            for j in range(nb):
                qparts, kparts = [], []
                for i in range(nb):
                    rs = slice(i * blk, (i + 1) * blk)
                    if i < j:
                        qparts.append(jnp.zeros((blk, d), F32))
                    elif i == j:
                        qparts.append(qd_h[rs])
                    else:
                        qparts.append(qd_h[rs] * jnp.exp(mids[i][:, ls] - mids[j][:, ls]))
                    kparts.append(kd_h[rs] if i == j else jnp.zeros((blk, d), F32))
                qcols.append(jnp.concatenate(qparts, axis=0))
                kcols.append(jnp.concatenate(kparts, axis=0))
            qcat = jnp.concatenate(qcols, axis=1).astype(BF16)
            kcat = jnp.concatenate(kcols, axis=1).astype(BF16)
            att = lax.dot_general(qcat, kcat, (((1,), (1,)), ((), ())), preferred_element_type=F32)
            att = jnp.where(tri, att, 0.0).astype(BF16)
            v_h = vb[:, ls]
            s_t = st_ref[h]
            o = jnp.dot(att, v_h, preferred_element_type=F32)
            o = o + lax.dot_general(q_in[:, ls].astype(BF16), s_t.astype(BF16),
                                    (((1,), (1,)), ((), ())), preferred_element_type=F32)
            upd = lax.dot_general(v_h, k_end[:, ls].astype(BF16), (((0,), (0,)), ((), ())),
                                  preferred_element_type=F32)
            st_ref[h] = s_t * g_end[:, ls] + upd
            o = o * lax.rsqrt(jnp.mean(o * o, axis=-1, keepdims=True) + EPS)
            outs.append(o)
        o_all = jnp.concatenate(outs, axis=1) if hb > 1 else outs[0]
        o_ref[0, rows, :] = (o_all * ngv * gate).astype(BF16)
        return carry

    lax.fori_loop(0, tt // ch, chunk, 0)

    @pl.when(t == pl.num_programs(2) - 1)
    def _():
        for h in range(hb):
            sn_ref[0, h] = st_ref[h].T


def _hgrn_group(p3, s0, lb_logits, norm_g, col0, layer):
    b, t, _ = p3.shape
    heads = s0.shape[1]
    d = HEAD_DIM
    hdim = heads * d
    hb = 2
    tt = _pick(t, (512, 256, 128, 64, 32, 16))
    ch = min(tt, 64)
    blk = min(ch, 32)
    lw = hb * d
    qb, fb, ib, gb = [(col0 + g * hdim) // lw for g in range(4)]
    kern = functools.partial(_hgrn_kernel, tt=tt, ch=ch, blk=blk, hb=hb, layer=layer)

    def col_spec(cb):
        return pl.BlockSpec((1, tt, lw), lambda i, h, j: (i, j, cb + h))

    return pl.pallas_call(
        kern,
        grid=(b, heads // hb, t // tt),
        in_specs=[
            col_spec(qb), col_spec(fb), col_spec(ib), col_spec(gb),
            pl.BlockSpec((1, hb, d, d), lambda i, h, j: (i, h, 0, 0)),
            pl.BlockSpec((lb_logits.shape[0], lw), lambda i, h, j: (0, h)),
            pl.BlockSpec((1, lw), lambda i, h, j: (0, h)),
        ],
        out_specs=[
            pl.BlockSpec((1, tt, lw), lambda i, h, j: (i, j, h)),
            pl.BlockSpec((1, hb, d, d), lambda i, h, j: (i, h, 0, 0)),
        ],
        out_shape=[
            jax.ShapeDtypeStruct((b, t, hdim), BF16),
            jax.ShapeDtypeStruct(s0.shape, F32),
        ],
        scratch_shapes=[pltpu.VMEM((hb, d, d), F32)],
        compiler_params=_params("parallel", "parallel", "arbitrary"),
        name="hgrn_group",
    )(p3, p3, p3, p3, s0, lb_logits, norm_g)


def _out_proj_kernel(x_ref, a_ref, b_ref, wa_ref, wb_ref, o_ref):
    o_ref[...] = (x_ref[...]
                  + jnp.dot(a_ref[...], wa_ref[...], preferred_element_type=F32)
                  + jnp.dot(b_ref[...], wb_ref[...], preferred_element_type=F32))


def _out_proj(x, a, bmix, w):
    n, d = x.shape
    ca, cb = a.shape[1], bmix.shape[1]
    assert ca == cb
    tm = _pick(n, (512, 256))
    return pl.pallas_call(
        _out_proj_kernel,
        grid=(n // tm,),
        in_specs=[
            pl.BlockSpec((tm, d), lambda i: (i, 0)),
            pl.BlockSpec((tm, ca), lambda i: (i, 0)),
            pl.BlockSpec((tm, cb), lambda i: (i, 0)),
            pl.BlockSpec((ca, d), lambda i: (0, 0)),
            pl.BlockSpec((cb, d), lambda i: (1, 0)),
        ],
        out_specs=pl.BlockSpec((tm, d), lambda i: (i, 0)),
        out_shape=jax.ShapeDtypeStruct((n, d), F32),
        compiler_params=_params("parallel"),
        name="out_proj",
    )(x, a, bmix, w, w)


def _mlp_kernel(h_ref, g_ref, wu_ref, wd_ref, gf_ref, o_ref, m_ref):
    j = pl.program_id(1)

    @pl.when(j == 0)
    def _():
        h = h_ref[...]
        ms = jnp.mean(h * h, axis=-1, keepdims=True)
        m_ref[...] = (h * lax.rsqrt(ms + EPS) * g_ref[...]).astype(BF16)
        o_ref[...] = h

    u = jnp.dot(m_ref[...], wu_ref[...], preferred_element_type=F32)
    r = jnp.square(jnp.maximum(u, 0.0)).astype(BF16)
    o_ref[...] += jnp.dot(r, wd_ref[...], preferred_element_type=F32)

    @pl.when(j == pl.num_programs(1) - 1)
    def _():
        y = o_ref[...]
        ms = jnp.mean(y * y, axis=-1, keepdims=True)
        o_ref[...] = y * lax.rsqrt(ms + EPS) * gf_ref[...]


def _mlp(h, g, w_up, w_down, g_final):
    n, d = h.shape
    dff = w_up.shape[1]
    tm = _pick(n, (1024, 512, 256))
    tf = _pick(dff, (512, 256))
    return pl.pallas_call(
        _mlp_kernel,
        grid=(n // tm, dff // tf),
        in_specs=[
            pl.BlockSpec((tm, d), lambda i, j: (i, 0)),
            pl.BlockSpec((1, d), lambda i, j: (0, 0)),
            pl.BlockSpec((d, tf), lambda i, j: (0, j)),
            pl.BlockSpec((tf, d), lambda i, j: (j, 0)),
            pl.BlockSpec((1, d), lambda i, j: (0, 0)),
        ],
        out_specs=pl.BlockSpec((tm, d), lambda i, j: (i, 0)),
        out_shape=jax.ShapeDtypeStruct((n, d), F32),
        scratch_shapes=[pltpu.VMEM((tm, d), BF16)],
        compiler_params=_params("parallel", "arbitrary"),
        name="mlp",
    )(h, g, w_up, w_down, g_final)


def _layer(x, conv_state, s0, lb_logits, layer, g_mix, w_in, w_dw, b_dw, ln_g, ln_b, hgrn_g, w_out,
           g_mlp, w_up, w_down, g_final):
    b, t, d = x.shape
    cdim = w_dw.shape[1]
    x2 = x.reshape(b * t, d)
    p = _in_proj(x2, g_mix, w_in)
    p3 = p.reshape(b, t, p.shape[1])
    a_out, new_conv = _conv_group(p3, conv_state, w_dw, b_dw, ln_g, ln_b)
    b_out, new_s = _hgrn_group(p3, s0, lb_logits, hgrn_g, 2 * cdim, layer)
    h = _out_proj(x2, a_out.reshape(b * t, -1), b_out.reshape(b * t, -1), w_out)
    y = _mlp(h, g_mlp, w_up, w_down, g_final)
    return y.reshape(b, t, d), new_conv, new_s


def kernel(x_prompt, x_sample, state_conv, state_hgrn, norm_mix_g, w_in, w_dw, b_dw, ln_conv_g, ln_conv_b,
           lb_logits, hgrn_norm_g, w_out, norm_mlp_g, w_up, w_down, norm_final_g):
    depth = w_in.shape[0]
    assert depth == 1, "single-layer trunk"
    l = 0
    bp = x_prompt.shape[0]
    weights = (norm_mix_g[l][None], w_in[l].astype(BF16), w_dw[l], b_dw[l][None], ln_conv_g[l][None],
               ln_conv_b[l][None], hgrn_norm_g[l][None], w_out[l].astype(BF16), norm_mlp_g[l][None],
               w_up[l].astype(BF16), w_down[l].astype(BF16), norm_final_g[None])
    zero_conv = jnp.zeros((bp,) + state_conv.shape[2:], state_conv.dtype)
    zero_s = jnp.zeros((bp,) + state_hgrn.shape[2:], state_hgrn.dtype)
    yp, cp, sp = _layer(x_prompt, zero_conv, zero_s, lb_logits, l, *weights)
    ys, cs, ss = _layer(x_sample, state_conv[l], state_hgrn[l], lb_logits, l, *weights)
    return (yp, ys, cp[None], sp[None], cs[None], ss[None])
```

```python
import functools

import jax
import jax.numpy as jnp
from jax import lax
from jax.experimental import pallas as pl
from jax.experimental.pallas import tpu as pltpu

EPS = 1e-6
HEAD_DIM = 128
CONV_HALO = 32
V7X_VMEM_LIMIT = 56 * 1024 * 1024

F32 = jnp.float32
BF16 = jnp.bfloat16


def _pick(n, candidates):
    for c in candidates:
        if n % c == 0:
            return c
    return n


def _params(*sem):
    return pltpu.CompilerParams(dimension_semantics=sem, vmem_limit_bytes=V7X_VMEM_LIMIT)


def _in_proj_kernel(x_ref, g_ref, w_ref, o_ref, n_ref):
    @pl.when(pl.program_id(1) == 0)
    def _():
        x = x_ref[...]
        ms = jnp.mean(x * x, axis=-1, keepdims=True)
        n_ref[...] = (x * lax.rsqrt(ms + EPS) * g_ref[...]).astype(BF16)

    o_ref[...] = jnp.dot(n_ref[...], w_ref[...], preferred_element_type=F32)


def _in_proj(x, g, w):
    n, d = x.shape
    cols = w.shape[1]
    tm = _pick(n, (1024, 512, 256))
    tn = _pick(cols, (1024, 512))
    return pl.pallas_call(
        _in_proj_kernel,
        grid=(n // tm, cols // tn),
        in_specs=[
            pl.BlockSpec((tm, d), lambda i, j: (i, 0)),
            pl.BlockSpec((1, d), lambda i, j: (0, 0)),
            pl.BlockSpec((d, tn), lambda i, j: (0, j)),
        ],
        out_specs=pl.BlockSpec((tm, tn), lambda i, j: (i, j)),
        out_shape=jax.ShapeDtypeStruct((n, cols), F32),
        scratch_shapes=[pltpu.VMEM((tm, d), BF16)],
        compiler_params=_params("parallel", "arbitrary"),
        name="in_proj",
    )(x, g, w)


def _conv_kernel(p_ref, st_ref, w_ref, b_ref, lg_ref, lb_ref, a_ref, ns_ref, ubuf, *, tt, rc, cdim, cw):
    t = pl.program_id(1)
    hist = cw - 1

    @pl.when(t == 0)
    def _():
        ubuf[0:CONV_HALO, :] = jnp.zeros((CONV_HALO, cdim), F32)
        ubuf[CONV_HALO - hist:CONV_HALO, :] = st_ref[0]

    @pl.when(t > 0)
    def _():
        ubuf[0:CONV_HALO, :] = ubuf[tt:tt + CONV_HALO, :]

    lanes = HEAD_DIM
    nlb = cdim // lanes

    def chunk(i, carry):
        base = pl.multiple_of(i * rc, rc)
        a = p_ref[0, pl.ds(base, rc), 0:cdim]
        gate = p_ref[0, pl.ds(base, rc), cdim:2 * cdim]
        ubuf[pl.ds(CONV_HALO + base, rc), :] = a * jax.nn.sigmoid(gate)
        pieces = []
        for lb in range(nlb):
            ls = slice(lb * lanes, (lb + 1) * lanes)
            acc = jnp.broadcast_to(b_ref[0:1, ls], (rc, lanes))
            win = ubuf[pl.ds(base, rc + CONV_HALO), ls]
            for res in range(8):
                offs = [o for o in range(CONV_HALO - hist, CONV_HALO + 1) if o % 8 == res]
                if not offs:
                    continue
                sh = pltpu.roll(win, rc + CONV_HALO - res, axis=0) if res else win
                for o in offs:
                    k = o - (CONV_HALO - hist)
                    acc = acc + sh[o - res:o - res + rc] * w_ref[k:k + 1, ls]
            pieces.append(acc)
        c = jnp.concatenate(pieces, axis=-1)
        mu = jnp.mean(c, axis=-1, keepdims=True)
        cc = c - mu
        var = jnp.mean(cc * cc, axis=-1, keepdims=True)
        cn = cc * lax.rsqrt(var + EPS) * lg_ref[...] + lb_ref[...]
        a_ref[0, pl.ds(base, rc), :] = _silu(cn).astype(BF16)
        return carry

    lax.fori_loop(0, tt // rc, chunk, 0)

    @pl.when(t == pl.num_programs(1) - 1)
    def _():
        ns_ref[0] = ubuf[tt + CONV_HALO - hist:tt + CONV_HALO, :]


def _conv_group(p3, state, w_dw, b_dw, ln_g, ln_b):
    b, t, _ = p3.shape
    cw, cdim = w_dw.shape
    hist = cw - 1
    assert hist <= CONV_HALO
    tt = _pick(t, (256, 128, 64, 32, 16))
    rc = min(tt, 64)
    kern = functools.partial(_conv_kernel, tt=tt, rc=rc, cdim=cdim, cw=cw)
    return pl.pallas_call(
        kern,
        grid=(b, t // tt),
        in_specs=[
            pl.BlockSpec((1, tt, 2 * cdim), lambda i, j: (i, j, 0)),
            pl.BlockSpec((1, hist, cdim), lambda i, j: (i, 0, 0)),
            pl.BlockSpec((cw, cdim), lambda i, j: (0, 0)),
            pl.BlockSpec((1, cdim), lambda i, j: (0, 0)),
            pl.BlockSpec((1, cdim), lambda i, j: (0, 0)),
            pl.BlockSpec((1, cdim), lambda i, j: (0, 0)),
        ],
        out_specs=[
            pl.BlockSpec((1, tt, cdim), lambda i, j: (i, j, 0)),
            pl.BlockSpec((1, hist, cdim), lambda i, j: (i, 0, 0)),
        ],
        out_shape=[
            jax.ShapeDtypeStruct((b, t, cdim), BF16),
            jax.ShapeDtypeStruct((b, hist, cdim), F32),
        ],
        scratch_shapes=[pltpu.VMEM((CONV_HALO + max(tt, CONV_HALO), cdim), F32)],
        compiler_params=_params("parallel", "arbitrary"),
        name="conv_group",
    )(p3, state, w_dw, b_dw, ln_g, ln_b)


def _silu(x):
    h = 0.5 * x
    return h + h * jnp.tanh(h)


def _cumsum_rows(x, tri3):
    hi = x.astype(BF16)
    r1 = x - hi.astype(F32)
    mid = r1.astype(BF16)
    lo = (r1 - mid.astype(F32)).astype(BF16)
    return jnp.dot(tri3, jnp.concatenate([hi, mid, lo], axis=0), preferred_element_type=F32)


def _hgrn_kernel(q_ref, f_ref, i_ref, g_ref, s0_ref, lbl_ref, ng_ref, o_ref, sn_ref, st_ref,
                 *, tt, ch, blk, hb, layer):
    t = pl.program_id(2)
    d = HEAD_DIM
    nb = ch // blk
    half = blk // 2

    @pl.when(t == 0)
    def _():
        for h in range(hb):
            st_ref[h] = s0_ref[0, h].T

    lg = lbl_ref[...]
    e = jnp.exp(lg - jnp.max(lg, axis=0, keepdims=True))
    lbv = jnp.sum(e[0:layer + 1], axis=0, keepdims=True) / jnp.sum(e, axis=0, keepdims=True)
    f_off = 0.5 * (1.0 + lbv)
    f_amp = 0.5 * (1.0 - lbv)
    ngv = ng_ref[...]

    r_i = lax.broadcasted_iota(jnp.int32, (ch, ch), 0)
    c_i = lax.broadcasted_iota(jnp.int32, (ch, ch), 1)
    tri = r_i >= c_i
    tri_b = jnp.where(tri, 1.0, 0.0).astype(BF16)
    tri3 = jnp.concatenate([tri_b, tri_b, tri_b], axis=1)
    zeros_b = jnp.zeros((blk, d), BF16)

    def bcast_rows(vals):
        return jnp.concatenate([jnp.broadcast_to(m, (blk, m.shape[1])) for m in vals], axis=0)

    def decay(c):
        rows = pl.ds(c * ch, ch)
        f = f_off + f_amp * jnp.tanh(0.5 * f_ref[0, rows, :])
        return f, _cumsum_rows(jnp.log2(f), tri3)

    def scores(c, f, cum):
        rows = pl.ds(c * ch, ch)
        mids = [cum[j * blk + half - 1:j * blk + half, :] for j in range(nb)]
        last = cum[ch - 1:ch, :]
        mid_rows = bcast_rows(mids)
        from_start = bcast_rows([jnp.exp2(m) for m in mids])
        to_end = bcast_rows([jnp.exp2(last - m) for m in mids])
        g_end = jnp.exp2(last)
        qd = _silu(q_ref[0, rows, :]) * jnp.exp2(cum - mid_rows)
        kd = (1.0 - f) * jnp.exp2(mid_rows - cum)
        qd_b = qd.astype(BF16)
        kd_b = kd.astype(BF16)
        q_in = (qd * from_start).astype(BF16)
        k_end = (kd * to_end).astype(BF16)
        vb = i_ref[0, rows, :].astype(BF16)
        cross = {}
        for i in range(nb):
            for j in range(i):
                scale = jnp.broadcast_to(jnp.exp2(mids[i] - mids[j]), (blk, mids[i].shape[1]))
                cross[i, j] = (qd[i * blk:(i + 1) * blk] * scale).astype(BF16)
        atts = []
        for h in range(hb):
            ls = slice(h * d, (h + 1) * d)
            qcols, kcols = [], []
            for j in range(nb):
                qparts, kparts = [], []
                for i in range(nb):
                    rs = slice(i * blk, (i + 1) * blk)
                    if i < j:
                        qparts.append(zeros_b)
                    elif i == j:
                        qparts.append(qd_b[rs, ls])
                    else:
                        qparts.append(cross[i, j][:, ls])
                    kparts.append(kd_b[rs, ls] if i == j else zeros_b)
                qcols.append(jnp.concatenate(qparts, axis=0))
                kcols.append(jnp.concatenate(kparts, axis=0))
            qcat = jnp.concatenate(qcols, axis=1)
            kcat = jnp.concatenate(kcols, axis=1)
            atts.append(lax.dot_general(qcat, kcat, (((1,), (1,)), ((), ())), preferred_element_type=F32))
        return atts, q_in, k_end, vb, g_end

    def output(c, atts, q_in, k_end, vb, g_end):
        rows = pl.ds(c * ch, ch)
        heads = [slice(h * d, (h + 1) * d) for h in range(hb)]
        states = [st_ref[h] for h in range(hb)]
        outs = []
        for h, ls in enumerate(heads):
            att = jnp.where(tri, atts[h], 0.0).astype(BF16)
            o = jnp.dot(att, vb[:, ls], preferred_element_type=F32)
            outs.append(o + lax.dot_general(q_in[:, ls], states[h].astype(BF16),
                                            (((1,), (1,)), ((), ())), preferred_element_type=F32))
        for h, ls in enumerate(heads):
            upd = lax.dot_general(vb[:, ls], k_end[:, ls], (((0,), (0,)), ((), ())),
                                  preferred_element_type=F32)
            st_ref[h] = states[h] * g_end[:, ls] + upd
        outs = [o * lax.rsqrt(jnp.mean(o * o, axis=-1, keepdims=True) + EPS) for o in outs]
        o_all = jnp.concatenate(outs, axis=1) if hb > 1 else outs[0]
        o_ref[0, rows, :] = (o_all * ngv * _silu(g_ref[0, rows, :])).astype(BF16)

    n_chunks = tt // ch
    staged = scores(0, *decay(0))
    for c in range(n_chunks):
        nxt = decay(c + 1) if c + 1 < n_chunks else None
        output(c, *staged)
        if nxt is not None:
            staged = scores(c + 1, *nxt)

    @pl.when(t == pl.num_programs(2) - 1)
    def _():
        for h in range(hb):
            sn_ref[0, h] = st_ref[h].T


def _hgrn_group(p3, s0, lb_logits, norm_g, col0, layer):
    b, t, _ = p3.shape
    heads = s0.shape[1]
    d = HEAD_DIM
    hdim = heads * d
    hb = heads
    tt = _pick(t, (512, 256, 128, 64, 32, 16))
    ch = min(tt, 64)
    blk = min(ch, 32)
    lw = hb * d
    qb, fb, ib, gb = [(col0 + g * hdim) // lw for g in range(4)]
    kern = functools.partial(_hgrn_kernel, tt=tt, ch=ch, blk=blk, hb=hb, layer=layer)

    def col_spec(cb):
        return pl.BlockSpec((1, tt, lw), lambda i, h, j: (i, j, cb + h))

    return pl.pallas_call(
        kern,
        grid=(b, heads // hb, t // tt),
        in_specs=[
            col_spec(qb), col_spec(fb), col_spec(ib), col_spec(gb),
            pl.BlockSpec((1, hb, d, d), lambda i, h, j: (i, h, 0, 0)),
            pl.BlockSpec((lb_logits.shape[0], lw), lambda i, h, j: (0, h)),
            pl.BlockSpec((1, lw), lambda i, h, j: (0, h)),
        ],
        out_specs=[
            pl.BlockSpec((1, tt, lw), lambda i, h, j: (i, j, h)),
            pl.BlockSpec((1, hb, d, d), lambda i, h, j: (i, h, 0, 0)),
        ],
        out_shape=[
            jax.ShapeDtypeStruct((b, t, hdim), BF16),
            jax.ShapeDtypeStruct(s0.shape, F32),
        ],
        scratch_shapes=[pltpu.VMEM((hb, d, d), F32)],
        compiler_params=_params("parallel", "parallel", "arbitrary"),
        name="hgrn_group",
    )(p3, p3, p3, p3, s0, lb_logits, norm_g)


def _out_proj_kernel(x_ref, a_ref, b_ref, wa_ref, wb_ref, o_ref):
    o_ref[...] = (x_ref[...]
                  + jnp.dot(a_ref[...], wa_ref[...], preferred_element_type=F32)
                  + jnp.dot(b_ref[...], wb_ref[...], preferred_element_type=F32))


def _out_proj(x, a, bmix, w):
    n, d = x.shape
    ca, cb = a.shape[1], bmix.shape[1]
    assert ca == cb
    tm = _pick(n, (512, 256))
    return pl.pallas_call(
        _out_proj_kernel,
        grid=(n // tm,),
        in_specs=[
            pl.BlockSpec((tm, d), lambda i: (i, 0)),
            pl.BlockSpec((tm, ca), lambda i: (i, 0)),
            pl.BlockSpec((tm, cb), lambda i: (i, 0)),
            pl.BlockSpec((ca, d), lambda i: (0, 0)),
            pl.BlockSpec((cb, d), lambda i: (1, 0)),
        ],
        out_specs=pl.BlockSpec((tm, d), lambda i: (i, 0)),
        out_shape=jax.ShapeDtypeStruct((n, d), F32),
        compiler_params=_params("parallel"),
        name="out_proj",
    )(x, a, bmix, w, w)


def _mlp_kernel(h_ref, g_ref, wu_ref, wd_ref, gf_ref, o_ref, m_ref):
    j = pl.program_id(1)

    @pl.when(j == 0)
    def _():
        h = h_ref[...]
        ms = jnp.mean(h * h, axis=-1, keepdims=True)
        m_ref[...] = (h * lax.rsqrt(ms + EPS) * g_ref[...]).astype(BF16)
        o_ref[...] = h

    u = jnp.dot(m_ref[...], wu_ref[...], preferred_element_type=F32)
    r = jnp.square(jnp.maximum(u, 0.0)).astype(BF16)
    o_ref[...] += jnp.dot(r, wd_ref[...], preferred_element_type=F32)

    @pl.when(j == pl.num_programs(1) - 1)
    def _():
        y = o_ref[...]
        ms = jnp.mean(y * y, axis=-1, keepdims=True)
        o_ref[...] = y * lax.rsqrt(ms + EPS) * gf_ref[...]


def _mlp(h, g, w_up, w_down, g_final):
    n, d = h.shape
    dff = w_up.shape[1]
    tm = _pick(n, (1024, 512, 256))
    tf = _pick(dff, (512, 256))
    return pl.pallas_call(
        _mlp_kernel,
        grid=(n // tm, dff // tf),
        in_specs=[
            pl.BlockSpec((tm, d), lambda i, j: (i, 0)),
            pl.BlockSpec((1, d), lambda i, j: (0, 0)),
            pl.BlockSpec((d, tf), lambda i, j: (0, j)),
            pl.BlockSpec((tf, d), lambda i, j: (j, 0)),
            pl.BlockSpec((1, d), lambda i, j: (0, 0)),
        ],
        out_specs=pl.BlockSpec((tm, d), lambda i, j: (i, 0)),
        out_shape=jax.ShapeDtypeStruct((n, d), F32),
        scratch_shapes=[pltpu.VMEM((tm, d), BF16)],
        compiler_params=_params("parallel", "arbitrary"),
        name="mlp",
    )(h, g, w_up, w_down, g_final)


def _layer(x, conv_state, s0, lb_logits, layer, g_mix, w_in, w_dw, b_dw, ln_g, ln_b, hgrn_g, w_out,
           g_mlp, w_up, w_down, g_final):
    b, t, d = x.shape
    cdim = w_dw.shape[1]
    x2 = x.reshape(b * t, d)
    p = _in_proj(x2, g_mix, w_in)
    p3 = p.reshape(b, t, p.shape[1])
    a_out, new_conv = _conv_group(p3, conv_state, w_dw, b_dw, ln_g, ln_b)
    b_out, new_s = _hgrn_group(p3, s0, lb_logits, hgrn_g, 2 * cdim, layer)
    h = _out_proj(x2, a_out.reshape(b * t, -1), b_out.reshape(b * t, -1), w_out)
    y = _mlp(h, g_mlp, w_up, w_down, g_final)
    return y.reshape(b, t, d), new_conv, new_s


def kernel(x_prompt, x_sample, state_conv, state_hgrn, norm_mix_g, w_in, w_dw, b_dw, ln_conv_g, ln_conv_b,
           lb_logits, hgrn_norm_g, w_out, norm_mlp_g, w_up, w_down, norm_final_g):
    depth = w_in.shape[0]
    assert depth == 1, "single-layer trunk"
    l = 0
    bp = x_prompt.shape[0]
    weights = (norm_mix_g[l][None], w_in[l].astype(BF16), w_dw[l], b_dw[l][None], ln_conv_g[l][None],
               ln_conv_b[l][None], hgrn_norm_g[l][None], w_out[l].astype(BF16), norm_mlp_g[l][None],
               w_up[l].astype(BF16), w_down[l].astype(BF16), norm_final_g[None])
    zero_conv = jnp.zeros((bp,) + state_conv.shape[2:], state_conv.dtype)
    zero_s = jnp.zeros((bp,) + state_hgrn.shape[2:], state_hgrn.dtype)
    yp, cp, sp = _layer(x_prompt, zero_conv, zero_s, lb_logits, l, *weights)
    ys, cs, ss = _layer(x_sample, state_conv[l], state_hgrn[l], lb_logits, l, *weights)
    return (yp, ys, cp[None], sp[None], cs[None], ss[None])
```

```python
import functools

import jax
import jax.numpy as jnp
from jax import lax
from jax.experimental import pallas as pl
from jax.experimental.pallas import tpu as pltpu

EPS = 1e-6
HEAD_DIM = 128
CONV_HALO = 32
FUSED_TIME_TILE = 1024
V7X_VMEM_LIMIT = 56 * 1024 * 1024

F32 = jnp.float32
BF16 = jnp.bfloat16


def _pick(n, candidates):
    for c in candidates:
        if n % c == 0:
            return c
    return n


def _params(*sem):
    return pltpu.CompilerParams(dimension_semantics=sem, vmem_limit_bytes=V7X_VMEM_LIMIT)


def _in_proj_kernel(x_ref, g_ref, w_ref, o_ref, n_ref):
    @pl.when(pl.program_id(1) == 0)
    def _():
        x = x_ref[...]
        ms = jnp.mean(x * x, axis=-1, keepdims=True)
        n_ref[...] = (x * lax.rsqrt(ms + EPS) * g_ref[...]).astype(BF16)

    o_ref[...] = jnp.dot(n_ref[...], w_ref[...], preferred_element_type=F32)


def _in_proj(x, g, w):
    n, d = x.shape
    cols = w.shape[1]
    tm = _pick(n, (1024, 512, 256))
    tn = _pick(cols, (3072, 1024, 512) if tm <= 256 else (1024, 512))
    return pl.pallas_call(
        _in_proj_kernel,
        grid=(n // tm, cols // tn),
        in_specs=[
            pl.BlockSpec((tm, d), lambda i, j: (i, 0)),
            pl.BlockSpec((1, d), lambda i, j: (0, 0)),
            pl.BlockSpec((d, tn), lambda i, j: (0, j)),
        ],
        out_specs=pl.BlockSpec((tm, tn), lambda i, j: (i, j)),
        out_shape=jax.ShapeDtypeStruct((n, cols), F32),
        scratch_shapes=[pltpu.VMEM((tm, d), BF16)],
        compiler_params=_params("parallel", "arbitrary"),
        name="in_proj",
    )(x, g, w)


def _sigmoid(x):
    return 0.5 + 0.5 * jnp.tanh(0.5 * x)


def _silu(x):
    h = 0.5 * x
    return h + h * jnp.tanh(h)


def _conv_rows(ubuf, base, rc, w_ref, b_ref, lg_ref, lb_ref, cdim, cw):
    hist = cw - 1
    lanes = HEAD_DIM
    pieces = []
    for lb in range(cdim // lanes):
        ls = slice(lb * lanes, (lb + 1) * lanes)
        acc = jnp.broadcast_to(b_ref[0:1, ls], (rc, lanes))
        win = ubuf[pl.ds(base, rc + CONV_HALO), ls]
        for res in range(8):
            offs = [o for o in range(CONV_HALO - hist, CONV_HALO + 1) if o % 8 == res]
            if not offs:
                continue
            sh = pltpu.roll(win, rc + CONV_HALO - res, axis=0) if res else win
            for o in offs:
                k = o - (CONV_HALO - hist)
                acc = acc + sh[o - res:o - res + rc] * w_ref[k:k + 1, ls]
        pieces.append(acc)
    c = jnp.concatenate(pieces, axis=-1)
    mu = jnp.mean(c, axis=-1, keepdims=True)
    cc = c - mu
    var = jnp.mean(cc * cc, axis=-1, keepdims=True)
    cn = cc * lax.rsqrt(var + EPS) * lg_ref[...] + lb_ref[...]
    return _silu(cn).astype(BF16)


def _conv_kernel(p_ref, st_ref, w_ref, b_ref, lg_ref, lb_ref, a_ref, ns_ref, ubuf, *, tt, rc, cdim, cw):
    t = pl.program_id(1)
    hist = cw - 1

    @pl.when(t == 0)
    def _():
        ubuf[0:CONV_HALO, :] = jnp.zeros((CONV_HALO, cdim), F32)
        ubuf[CONV_HALO - hist:CONV_HALO, :] = st_ref[0]

    @pl.when(t > 0)
    def _():
        ubuf[0:CONV_HALO, :] = ubuf[tt:tt + CONV_HALO, :]

    def chunk(i, carry):
        base = pl.multiple_of(i * rc, rc)
        a = p_ref[0, pl.ds(base, rc), 0:cdim]
        gate = p_ref[0, pl.ds(base, rc), cdim:2 * cdim]
        ubuf[pl.ds(CONV_HALO + base, rc), :] = a * _sigmoid(gate)
        a_ref[0, pl.ds(base, rc), :] = _conv_rows(ubuf, base, rc, w_ref, b_ref, lg_ref, lb_ref, cdim, cw)
        return carry

    lax.fori_loop(0, tt // rc, chunk, 0)

    @pl.when(t == pl.num_programs(1) - 1)
    def _():
        ns_ref[0] = ubuf[tt + CONV_HALO - hist:tt + CONV_HALO, :]


def _conv_group(p3, state, w_dw, b_dw, ln_g, ln_b):
    b, t, _ = p3.shape
    cw, cdim = w_dw.shape
    hist = cw - 1
    assert hist <= CONV_HALO
    tt = _pick(t, (256, 128, 64, 32, 16))
    rc = min(tt, 64)
    kern = functools.partial(_conv_kernel, tt=tt, rc=rc, cdim=cdim, cw=cw)
    return pl.pallas_call(
        kern,
        grid=(b, t // tt),
        in_specs=[
            pl.BlockSpec((1, tt, 2 * cdim), lambda i, j: (i, j, 0)),
            pl.BlockSpec((1, hist, cdim), lambda i, j: (i, 0, 0)),
            pl.BlockSpec((cw, cdim), lambda i, j: (0, 0)),
            pl.BlockSpec((1, cdim), lambda i, j: (0, 0)),
            pl.BlockSpec((1, cdim), lambda i, j: (0, 0)),
            pl.BlockSpec((1, cdim), lambda i, j: (0, 0)),
        ],
        out_specs=[
            pl.BlockSpec((1, tt, cdim), lambda i, j: (i, j, 0)),
            pl.BlockSpec((1, hist, cdim), lambda i, j: (i, 0, 0)),
        ],
        out_shape=[
            jax.ShapeDtypeStruct((b, t, cdim), BF16),
            jax.ShapeDtypeStruct((b, hist, cdim), F32),
        ],
        scratch_shapes=[pltpu.VMEM((CONV_HALO + max(tt, CONV_HALO), cdim), F32)],
        compiler_params=_params("parallel", "arbitrary"),
        name="conv_group",
    )(p3, state, w_dw, b_dw, ln_g, ln_b)


def _in_proj_conv_kernel(x_ref, g_ref, w_ref, st_ref, wdw_ref, bdw_ref, lg_ref, lb_ref,
                         p_ref, a_ref, ns_ref, n_ref, ubuf, *, tm, tps, rc, cdim, cw):
    j = pl.program_id(1)
    t = pl.program_id(0) % tps
    hist = cw - 1
    n_conv_steps = pl.num_programs(1) - 2
    rows_per_step = tm // 4

    @pl.when(j == 0)
    def _():
        @pl.when(t == 0)
        def _():
            ubuf[0:CONV_HALO, :] = jnp.zeros((CONV_HALO, cdim), F32)
            ubuf[CONV_HALO - hist:CONV_HALO, :] = st_ref[0]

        @pl.when(t > 0)
        def _():
            ubuf[0:CONV_HALO, :] = ubuf[tm:tm + CONV_HALO, :]

        x = x_ref[...]
        ms = jnp.mean(x * x, axis=-1, keepdims=True)
        n_ref[...] = (x * lax.rsqrt(ms + EPS) * g_ref[...]).astype(BF16)
        ubuf[CONV_HALO:CONV_HALO + tm, :] = _sigmoid(jnp.dot(n_ref[...], w_ref[...], preferred_element_type=F32))

    @pl.when(j == 1)
    def _():
        ubuf[CONV_HALO:CONV_HALO + tm, :] = (
            jnp.dot(n_ref[...], w_ref[...], preferred_element_type=F32) * ubuf[CONV_HALO:CONV_HALO + tm, :])

    @pl.when(j >= 2)
    def _():
        n_chunks = rows_per_step // rc
        wcols = w_ref.shape[1] // n_chunks
        for c in range(n_chunks):
            cs = slice(c * wcols, (c + 1) * wcols)
            p_ref[:, cs] = jnp.dot(n_ref[...], w_ref[:, cs], preferred_element_type=F32)
            base = pl.multiple_of((j - 2) * rows_per_step + c * rc, rc)
            a_ref[pl.ds(base, rc), :] = _conv_rows(ubuf, base, rc, wdw_ref, bdw_ref, lg_ref, lb_ref, cdim, cw)

        @pl.when((j == n_conv_steps + 1) & (t == tps - 1))
        def _():
            ns_ref[0] = ubuf[tm + CONV_HALO - hist:tm + CONV_HALO, :]


def _in_proj_conv(x3, g, w, state, w_dw, b_dw, ln_g, ln_b, tm):
    b, t, d = x3.shape
    cw, cdim = w_dw.shape
    hist = cw - 1
    cols = w.shape[1]
    tn = cdim
    nj = cols // tn
    assert hist <= CONV_HALO and t % tm == 0 and nj == 6 and tm % 256 == 0
    tps = t // tm
    kern = functools.partial(_in_proj_conv_kernel, tm=tm, tps=tps, rc=64, cdim=cdim, cw=cw)
    const = lambda i, j: (0, 0)
    return pl.pallas_call(
        kern,
        grid=(b * tps, nj),
        in_specs=[
            pl.BlockSpec((tm, d), lambda i, j: (i, 0)),
            pl.BlockSpec((1, d), const),
            pl.BlockSpec((d, tn), lambda i, j: (0, jnp.where(j < 2, 1 - j, j))),
            pl.BlockSpec((1, hist, cdim), lambda i, j: (i // tps, 0, 0)),
            pl.BlockSpec((cw, cdim), const),
            pl.BlockSpec((1, cdim), const),
            pl.BlockSpec((1, cdim), const),
            pl.BlockSpec((1, cdim), const),
        ],
        out_specs=[
            pl.BlockSpec((tm, tn), lambda i, j: (i, jnp.maximum(j - 2, 0))),
            pl.BlockSpec((tm, cdim), lambda i, j: (i, 0)),
            pl.BlockSpec((1, hist, cdim), lambda i, j: (i // tps, 0, 0)),
        ],
        out_shape=[
            jax.ShapeDtypeStruct((b * t, cols - 2 * cdim), F32),
            jax.ShapeDtypeStruct((b * t, cdim), BF16),
            jax.ShapeDtypeStruct((b, hist, cdim), F32),
        ],
        scratch_shapes=[
            pltpu.VMEM((tm, d), BF16),
            pltpu.VMEM((CONV_HALO + tm, cdim), F32),
        ],
        compiler_params=_params("arbitrary", "arbitrary"),
        name="in_proj_conv",
    )(x3.reshape(b * t, d), g, w, state, w_dw, b_dw, ln_g, ln_b)


def _cumsum_rows(x, tri3):
    hi = x.astype(BF16)
    r1 = x - hi.astype(F32)
    mid = r1.astype(BF16)
    lo = (r1 - mid.astype(F32)).astype(BF16)
    return jnp.dot(tri3, jnp.concatenate([hi, mid, lo], axis=0), preferred_element_type=F32)


def _hgrn_kernel(q_ref, f_ref, i_ref, g_ref, s0_ref, lbl_ref, ng_ref, o_ref, sn_ref, st_ref,
                 *, tt, ch, blk, hb, layer):
    t = pl.program_id(2)
    d = HEAD_DIM
    nb = ch // blk
    half = blk // 2

    @pl.when(t == 0)
    def _():
        for h in range(hb):
            st_ref[h] = s0_ref[0, h].T

    lg = lbl_ref[...]
    e = jnp.exp(lg - jnp.max(lg, axis=0, keepdims=True))
    lbv = jnp.sum(e[0:layer + 1], axis=0, keepdims=True) / jnp.sum(e, axis=0, keepdims=True)
    f_off = 0.5 * (1.0 + lbv)
    f_amp = 0.5 * (1.0 - lbv)
    ngv = ng_ref[...]

    r_i = lax.broadcasted_iota(jnp.int32, (ch, ch), 0)
    c_i = lax.broadcasted_iota(jnp.int32, (ch, ch), 1)
    tri = r_i >= c_i
    tri_b = jnp.where(tri, 1.0, 0.0).astype(BF16)
    tri3 = jnp.concatenate([tri_b, tri_b, tri_b], axis=1)
    zeros_b = jnp.zeros((blk, d), BF16)

    def bcast_rows(vals):
        return jnp.concatenate([jnp.broadcast_to(m, (blk, m.shape[1])) for m in vals], axis=0)

    def decay(c):
        rows = pl.ds(c * ch, ch)
        f = f_off + f_amp * jnp.tanh(0.5 * f_ref[0, rows, :])
        return f, _cumsum_rows(jnp.log2(f), tri3)

    def scores(c, f, cum):
        rows = pl.ds(c * ch, ch)
        mids = [cum[j * blk + half - 1:j * blk + half, :] for j in range(nb)]
        last = cum[ch - 1:ch, :]
        mid_rows = bcast_rows(mids)
        from_start = bcast_rows([jnp.exp2(m) for m in mids])
        to_end = bcast_rows([jnp.exp2(last - m) for m in mids])
        g_end = jnp.exp2(last)
        qd = _silu(q_ref[0, rows, :]) * jnp.exp2(cum - mid_rows)
        kd = (1.0 - f) * jnp.exp2(mid_rows - cum)
        qd_b = qd.astype(BF16)
        kd_b = kd.astype(BF16)
        q_in = (qd * from_start).astype(BF16)
        k_end = (kd * to_end).astype(BF16)
        vb = i_ref[0, rows, :].astype(BF16)
        cross = {}
        for i in range(nb):
            for j in range(i):
                scale = jnp.broadcast_to(jnp.exp2(mids[i] - mids[j]), (blk, mids[i].shape[1]))
                cross[i, j] = (qd[i * blk:(i + 1) * blk] * scale).astype(BF16)
        atts = []
        for h in range(hb):
            ls = slice(h * d, (h + 1) * d)
            qcols, kcols = [], []
            for j in range(nb):
                qparts, kparts = [], []
                for i in range(nb):
                    rs = slice(i * blk, (i + 1) * blk)
                    if i < j:
                        qparts.append(zeros_b)
                    elif i == j:
                        qparts.append(qd_b[rs, ls])
                    else:
                        qparts.append(cross[i, j][:, ls])
                    kparts.append(kd_b[rs, ls] if i == j else zeros_b)
                qcols.append(jnp.concatenate(qparts, axis=0))
                kcols.append(jnp.concatenate(kparts, axis=0))
            qcat = jnp.concatenate(qcols, axis=1)
            kcat = jnp.concatenate(kcols, axis=1)
            atts.append(lax.dot_general(qcat, kcat, (((1,), (1,)), ((), ())), preferred_element_type=F32))
        return atts, q_in, k_end, vb, g_end

    def output(c, atts, q_in, k_end, vb, g_end):
        rows = pl.ds(c * ch, ch)
        heads = [slice(h * d, (h + 1) * d) for h in range(hb)]
        states = [st_ref[h] for h in range(hb)]
        outs = []
        for h, ls in enumerate(heads):
            att = jnp.where(tri, atts[h], 0.0).astype(BF16)
            o = jnp.dot(att, vb[:, ls], preferred_element_type=F32)
            outs.append(o + lax.dot_general(q_in[:, ls], states[h].astype(BF16),
                                            (((1,), (1,)), ((), ())), preferred_element_type=F32))
        for h, ls in enumerate(heads):
            upd = lax.dot_general(vb[:, ls], k_end[:, ls], (((0,), (0,)), ((), ())),
                                  preferred_element_type=F32)
            st_ref[h] = states[h] * g_end[:, ls] + upd
        outs = [o * lax.rsqrt(jnp.mean(o * o, axis=-1, keepdims=True) + EPS) for o in outs]
        o_all = jnp.concatenate(outs, axis=1) if hb > 1 else outs[0]
        o_ref[0, rows, :] = (o_all * ngv * _silu(g_ref[0, rows, :])).astype(BF16)

    n_chunks = tt // ch
    staged = scores(0, *decay(0))
    for c in range(n_chunks):
        nxt = decay(c + 1) if c + 1 < n_chunks else None
        output(c, *staged)
        if nxt is not None:
            staged = scores(c + 1, *nxt)

    @pl.when(t == pl.num_programs(2) - 1)
    def _():
        for h in range(hb):
            sn_ref[0, h] = st_ref[h].T


def _hgrn_group(p3, s0, lb_logits, norm_g, col0, layer):
    b, t, _ = p3.shape
    heads = s0.shape[1]
    d = HEAD_DIM
    hdim = heads * d
    hb = heads
    tt = _pick(t, (512, 256, 128, 64, 32, 16))
    ch = min(tt, 64)
    blk = min(ch, 32)
    lw = hb * d
    qb, fb, ib, gb = [(col0 + g * hdim) // lw for g in range(4)]
    kern = functools.partial(_hgrn_kernel, tt=tt, ch=ch, blk=blk, hb=hb, layer=layer)

    def col_spec(cb):
        return pl.BlockSpec((1, tt, lw), lambda i, h, j: (i, j, cb + h))

    return pl.pallas_call(
        kern,
        grid=(b, heads // hb, t // tt),
        in_specs=[
            col_spec(qb), col_spec(fb), col_spec(ib), col_spec(gb),
            pl.BlockSpec((1, hb, d, d), lambda i, h, j: (i, h, 0, 0)),
            pl.BlockSpec((lb_logits.shape[0], lw), lambda i, h, j: (0, h)),
            pl.BlockSpec((1, lw), lambda i, h, j: (0, h)),
        ],
        out_specs=[
            pl.BlockSpec((1, tt, lw), lambda i, h, j: (i, j, h)),
            pl.BlockSpec((1, hb, d, d), lambda i, h, j: (i, h, 0, 0)),
        ],
        out_shape=[
            jax.ShapeDtypeStruct((b, t, hdim), BF16),
            jax.ShapeDtypeStruct(s0.shape, F32),
        ],
        scratch_shapes=[pltpu.VMEM((hb, d, d), F32)],
        compiler_params=_params("parallel", "parallel", "arbitrary"),
        name="hgrn_group",
    )(p3, p3, p3, p3, s0, lb_logits, norm_g)


def _out_proj_kernel(x_ref, a_ref, b_ref, wa_ref, wb_ref, o_ref):
    o_ref[...] = (x_ref[...]
                  + jnp.dot(a_ref[...], wa_ref[...], preferred_element_type=F32)
                  + jnp.dot(b_ref[...], wb_ref[...], preferred_element_type=F32))


def _out_proj(x, a, bmix, w):
    n, d = x.shape
    ca, cb = a.shape[1], bmix.shape[1]
    assert ca == cb
    tm = _pick(n, (512, 256))
    return pl.pallas_call(
        _out_proj_kernel,
        grid=(n // tm,),
        in_specs=[
            pl.BlockSpec((tm, d), lambda i: (i, 0)),
            pl.BlockSpec((tm, ca), lambda i: (i, 0)),
            pl.BlockSpec((tm, cb), lambda i: (i, 0)),
            pl.BlockSpec((ca, d), lambda i: (0, 0)),
            pl.BlockSpec((cb, d), lambda i: (1, 0)),
        ],
        out_specs=pl.BlockSpec((tm, d), lambda i: (i, 0)),
        out_shape=jax.ShapeDtypeStruct((n, d), F32),
        compiler_params=_params("parallel"),
        name="out_proj",
    )(x, a, bmix, w, w)


def _mlp_kernel(h_ref, g_ref, wu_ref, wd_ref, gf_ref, o_ref, m_ref):
    j = pl.program_id(1)

    @pl.when(j == 0)
    def _():
        h = h_ref[...]
        ms = jnp.mean(h * h, axis=-1, keepdims=True)
        m_ref[...] = (h * lax.rsqrt(ms + EPS) * g_ref[...]).astype(BF16)
        o_ref[...] = h

    u = jnp.dot(m_ref[...], wu_ref[...], preferred_element_type=F32)
    r = jnp.square(jnp.maximum(u, 0.0)).astype(BF16)
    o_ref[...] += jnp.dot(r, wd_ref[...], preferred_element_type=F32)

    @pl.when(j == pl.num_programs(1) - 1)
    def _():
        y = o_ref[...]
        ms = jnp.mean(y * y, axis=-1, keepdims=True)
        o_ref[...] = y * lax.rsqrt(ms + EPS) * gf_ref[...]


def _mlp(h, g, w_up, w_down, g_final):
    n, d = h.shape
    dff = w_up.shape[1]
    tm = _pick(n, (1024, 512, 256))
    tf = _pick(dff, (2048, 512, 256) if tm <= 256 else (512, 256))
    return pl.pallas_call(
        _mlp_kernel,
        grid=(n // tm, dff // tf),
        in_specs=[
            pl.BlockSpec((tm, d), lambda i, j: (i, 0)),
            pl.BlockSpec((1, d), lambda i, j: (0, 0)),
            pl.BlockSpec((d, tf), lambda i, j: (0, j)),
            pl.BlockSpec((tf, d), lambda i, j: (j, 0)),
            pl.BlockSpec((1, d), lambda i, j: (0, 0)),
        ],
        out_specs=pl.BlockSpec((tm, d), lambda i, j: (i, 0)),
        out_shape=jax.ShapeDtypeStruct((n, d), F32),
        scratch_shapes=[pltpu.VMEM((tm, d), BF16)],
        compiler_params=_params("parallel", "arbitrary"),
        name="mlp",
    )(h, g, w_up, w_down, g_final)


def _layer(x, conv_state, s0, lb_logits, layer, g_mix, w_in, w_dw, b_dw, ln_g, ln_b, hgrn_g, w_out,
           g_mlp, w_up, w_down, g_final):
    b, t, d = x.shape
    cdim = w_dw.shape[1]
    x2 = x.reshape(b * t, d)
    if t % FUSED_TIME_TILE == 0:
        p, a_out, new_conv = _in_proj_conv(x, g_mix, w_in, conv_state, w_dw, b_dw, ln_g, ln_b, FUSED_TIME_TILE)
        hgrn_col0 = 0
    else:
        p = _in_proj(x2, g_mix, w_in)
        a_out, new_conv = _conv_group(p.reshape(b, t, p.shape[1]), conv_state, w_dw, b_dw, ln_g, ln_b)
        hgrn_col0 = 2 * cdim
    b_out, new_s = _hgrn_group(p.reshape(b, t, p.shape[1]), s0, lb_logits, hgrn_g, hgrn_col0, layer)
    h = _out_proj(x2, a_out.reshape(b * t, -1), b_out.reshape(b * t, -1), w_out)
    y = _mlp(h, g_mlp, w_up, w_down, g_final)
    return y.reshape(b, t, d), new_conv, new_s


def kernel(x_prompt, x_sample, state_conv, state_hgrn, norm_mix_g, w_in, w_dw, b_dw, ln_conv_g, ln_conv_b,
           lb_logits, hgrn_norm_g, w_out, norm_mlp_g, w_up, w_down, norm_final_g):
    depth = w_in.shape[0]
    assert depth == 1, "single-layer trunk"
    l = 0
    bp = x_prompt.shape[0]
    weights = (norm_mix_g[l][None], w_in[l].astype(BF16), w_dw[l], b_dw[l][None], ln_conv_g[l][None],
               ln_conv_b[l][None], hgrn_norm_g[l][None], w_out[l].astype(BF16), norm_mlp_g[l][None],
               w_up[l].astype(BF16), w_down[l].astype(BF16), norm_final_g[None])
    zero_conv = jnp.zeros((bp,) + state_conv.shape[2:], state_conv.dtype)
    zero_s = jnp.zeros((bp,) + state_hgrn.shape[2:], state_hgrn.dtype)
    yp, cp, sp = _layer(x_prompt, zero_conv, zero_s, lb_logits, l, *weights)
    ys, cs, ss = _layer(x_sample, state_conv[l], state_hgrn[l], lb_logits, l, *weights)
    return (yp, ys, cp[None], sp[None], cs[None], ss[None])
```

```python
import functools

import jax
import jax.numpy as jnp
from jax import lax
from jax.experimental import pallas as pl
from jax.experimental.pallas import tpu as pltpu

EPS = 1e-6
HEAD_DIM = 128
CONV_HALO = 32
BF16_ROWS = 16
FUSED_TIME_TILE = 1024
V7X_VMEM_LIMIT = 60 * 1024 * 1024

F32 = jnp.float32
BF16 = jnp.bfloat16


def _pick(n, candidates):
    for c in candidates:
        if n % c == 0:
            return c
    return n


def _params(*sem):
    return pltpu.CompilerParams(dimension_semantics=sem, vmem_limit_bytes=V7X_VMEM_LIMIT)


def _in_proj_kernel(x_ref, g_ref, w_ref, o_ref, n_ref):
    @pl.when(pl.program_id(1) == 0)
    def _():
        x = x_ref[...]
        ms = jnp.mean(x * x, axis=-1, keepdims=True)
        n_ref[...] = (x * lax.rsqrt(ms + EPS) * g_ref[...]).astype(BF16)

    o_ref[...] = jnp.dot(n_ref[...], w_ref[...], preferred_element_type=F32)


def _in_proj(x, g, w):
    n, d = x.shape
    cols = w.shape[1]
    tm = _pick(n, (1024, 512, 256))
    tn = _pick(cols, (3072, 1024, 512) if tm <= 256 else (1024, 512))
    return pl.pallas_call(
        _in_proj_kernel,
        grid=(n // tm, cols // tn),
        in_specs=[
            pl.BlockSpec((tm, d), lambda i, j: (i, 0)),
            pl.BlockSpec((1, d), lambda i, j: (0, 0)),
            pl.BlockSpec((d, tn), lambda i, j: (0, j)),
        ],
        out_specs=pl.BlockSpec((tm, tn), lambda i, j: (i, j)),
        out_shape=jax.ShapeDtypeStruct((n, cols), F32),
        scratch_shapes=[pltpu.VMEM((tm, d), BF16)],
        compiler_params=_params("parallel", "arbitrary"),
        name="in_proj",
    )(x, g, w)


def _sigmoid(x):
    return 0.5 + 0.5 * jnp.tanh(0.5 * x)


def _silu(x):
    h = 0.5 * x
    return h + h * jnp.tanh(h)


def _conv_rows(ubuf, base, rc, w_ref, b_ref, lg_ref, lb_ref, cdim, cw):
    hist = cw - 1
    lanes = HEAD_DIM
    pieces = []
    for lb in range(cdim // lanes):
        ls = slice(lb * lanes, (lb + 1) * lanes)
        acc = jnp.broadcast_to(b_ref[0:1, ls], (rc, lanes))
        win = ubuf[pl.ds(base, rc + CONV_HALO), ls]
        for res in range(8):
            offs = [o for o in range(CONV_HALO - hist, CONV_HALO + 1) if o % 8 == res]
            if not offs:
                continue
            sh = pltpu.roll(win, rc + CONV_HALO - res, axis=0) if res else win
            for o in offs:
                k = o - (CONV_HALO - hist)
                acc = acc + sh[o - res:o - res + rc] * w_ref[k:k + 1, ls]
        pieces.append(acc)
    c = jnp.concatenate(pieces, axis=-1)
    mu = jnp.mean(c, axis=-1, keepdims=True)
    cc = c - mu
    var = jnp.mean(cc * cc, axis=-1, keepdims=True)
    cn = cc * lax.rsqrt(var + EPS) * lg_ref[...] + lb_ref[...]
    return _silu(cn).astype(BF16)


def _conv_kernel(p_ref, st_ref, w_ref, b_ref, lg_ref, lb_ref, a_ref, ns_ref, ubuf, *, tt, rc, cdim, cw):
    t = pl.program_id(1)
    hist = cw - 1

    @pl.when(t == 0)
    def _():
        ubuf[0:CONV_HALO, :] = jnp.zeros((CONV_HALO, cdim), F32)
        ubuf[CONV_HALO - hist:CONV_HALO, :] = st_ref[0]

    @pl.when(t > 0)
    def _():
        ubuf[0:CONV_HALO, :] = ubuf[tt:tt + CONV_HALO, :]

    def chunk(i, carry):
        base = pl.multiple_of(i * rc, rc)
        a = p_ref[0, pl.ds(base, rc), 0:cdim]
        gate = p_ref[0, pl.ds(base, rc), cdim:2 * cdim]
        ubuf[pl.ds(CONV_HALO + base, rc), :] = a * _sigmoid(gate)
        a_ref[0, pl.ds(base, rc), :] = _conv_rows(ubuf, base, rc, w_ref, b_ref, lg_ref, lb_ref, cdim, cw)
        return carry

    lax.fori_loop(0, tt // rc, chunk, 0)

    @pl.when(t == pl.num_programs(1) - 1)
    def _():
        ns_ref[0] = ubuf[tt + CONV_HALO - hist:tt + CONV_HALO, :]


def _conv_group(p3, state, w_dw, b_dw, ln_g, ln_b):
    b, t, _ = p3.shape
    cw, cdim = w_dw.shape
    hist = cw - 1
    assert hist <= CONV_HALO
    tt = _pick(t, (256, 128, 64, 32, 16))
    rc = min(tt, 64)
    kern = functools.partial(_conv_kernel, tt=tt, rc=rc, cdim=cdim, cw=cw)
    return pl.pallas_call(
        kern,
        grid=(b, t // tt),
        in_specs=[
            pl.BlockSpec((1, tt, 2 * cdim), lambda i, j: (i, j, 0)),
            pl.BlockSpec((1, hist, cdim), lambda i, j: (i, 0, 0)),
            pl.BlockSpec((cw, cdim), lambda i, j: (0, 0)),
            pl.BlockSpec((1, cdim), lambda i, j: (0, 0)),
            pl.BlockSpec((1, cdim), lambda i, j: (0, 0)),
            pl.BlockSpec((1, cdim), lambda i, j: (0, 0)),
        ],
        out_specs=[
            pl.BlockSpec((1, tt, cdim), lambda i, j: (i, j, 0)),
            pl.BlockSpec((1, hist, cdim), lambda i, j: (i, 0, 0)),
        ],
        out_shape=[
            jax.ShapeDtypeStruct((b, t, cdim), BF16),
            jax.ShapeDtypeStruct((b, hist, cdim), F32),
        ],
        scratch_shapes=[pltpu.VMEM((CONV_HALO + max(tt, CONV_HALO), cdim), F32)],
        compiler_params=_params("parallel", "arbitrary"),
        name="conv_group",
    )(p3, state, w_dw, b_dw, ln_g, ln_b)


def _in_proj_conv_kernel(x_ref, g_ref, w_ref, st_ref, wdw_ref, bdw_ref, lg_ref, lb_ref, wu_ref, wd_ref,
                         p_ref, a_ref, ns_ref, wub_ref, wdb_ref, n_ref, ubuf,
                         *, tm, tps, rc, cdim, cw, n_cast):
    j = pl.program_id(1)
    t = pl.program_id(0) % tps
    hist = cw - 1
    n_conv_steps = pl.num_programs(1) - 2
    rows_per_step = tm // 4

    @pl.when(pl.program_id(0) * pl.num_programs(1) + j < n_cast)
    def _():
        wub_ref[...] = wu_ref[...].astype(BF16)
        wdb_ref[...] = wd_ref[...].astype(BF16)

    @pl.when(j == 0)
    def _():
        @pl.when(t == 0)
        def _():
            ubuf[0:CONV_HALO, :] = jnp.zeros((CONV_HALO, cdim), F32)
            ubuf[CONV_HALO - hist:CONV_HALO, :] = st_ref[0]

        @pl.when(t > 0)
        def _():
            ubuf[0:CONV_HALO, :] = ubuf[tm:tm + CONV_HALO, :]

        x = x_ref[...]
        ms = jnp.mean(x * x, axis=-1, keepdims=True)
        n_ref[...] = (x * lax.rsqrt(ms + EPS) * g_ref[...]).astype(BF16)
        ubuf[CONV_HALO:CONV_HALO + tm, :] = _sigmoid(jnp.dot(n_ref[...], w_ref[...], preferred_element_type=F32))

    @pl.when(j == 1)
    def _():
        ubuf[CONV_HALO:CONV_HALO + tm, :] = (
            jnp.dot(n_ref[...], w_ref[...], preferred_element_type=F32) * ubuf[CONV_HALO:CONV_HALO + tm, :])

    @pl.when(j >= 2)
    def _():
        p_ref[...] = jnp.dot(n_ref[...], w_ref[...], preferred_element_type=F32)
        for c in range(rows_per_step // rc):
            base = pl.multiple_of((j - 2) * rows_per_step + c * rc, rc)
            a_ref[pl.ds(base, rc), :] = _conv_rows(ubuf, base, rc, wdw_ref, bdw_ref, lg_ref, lb_ref, cdim, cw)

        @pl.when((j == n_conv_steps + 1) & (t == tps - 1))
        def _():
            ns_ref[0] = ubuf[tm + CONV_HALO - hist:tm + CONV_HALO, :]


def _in_proj_conv(x3, g, w, state, w_dw, b_dw, ln_g, ln_b, w_up, w_down, tm):
    b, t, d = x3.shape
    cw, cdim = w_dw.shape
    hist = cw - 1
    cols = w.shape[1]
    tn = cdim
    nj = cols // tn
    assert hist <= CONV_HALO and t % tm == 0 and nj == 6 and tm % 256 == 0
    tps = t // tm
    n_cast = max(c for c in (64, 32, 16, 8, 4, 2, 1) if c <= b * tps * nj)
    up_rows, down_rows = w_up.shape[0] // n_cast, w_down.shape[0] // n_cast
    assert up_rows * n_cast == w_up.shape[0] and down_rows * n_cast == w_down.shape[0]
    assert up_rows % BF16_ROWS == 0 and down_rows % BF16_ROWS == 0
    cast_block = lambda i, j: (jnp.minimum(i * nj + j, n_cast - 1), 0)
    kern = functools.partial(_in_proj_conv_kernel, tm=tm, tps=tps, rc=64, cdim=cdim, cw=cw, n_cast=n_cast)
    const = lambda i, j: (0, 0)
    return pl.pallas_call(
        kern,
        grid=(b * tps, nj),
        in_specs=[
            pl.BlockSpec((tm, d), lambda i, j: (i, 0)),
            pl.BlockSpec((1, d), const),
            pl.BlockSpec((d, tn), lambda i, j: (0, jnp.where(j < 2, 1 - j, j))),
            pl.BlockSpec((1, hist, cdim), lambda i, j: (i // tps, 0, 0)),
            pl.BlockSpec((cw, cdim), const),
            pl.BlockSpec((1, cdim), const),
            pl.BlockSpec((1, cdim), const),
            pl.BlockSpec((1, cdim), const),
            pl.BlockSpec((up_rows, w_up.shape[1]), cast_block),
            pl.BlockSpec((down_rows, w_down.shape[1]), cast_block),
        ],
        out_specs=[
            pl.BlockSpec((tm, tn), lambda i, j: (i, jnp.maximum(j - 2, 0))),
            pl.BlockSpec((tm, cdim), lambda i, j: (i, 0)),
            pl.BlockSpec((1, hist, cdim), lambda i, j: (i // tps, 0, 0)),
            pl.BlockSpec((up_rows, w_up.shape[1]), cast_block),
            pl.BlockSpec((down_rows, w_down.shape[1]), cast_block),
        ],
        out_shape=[
            jax.ShapeDtypeStruct((b * t, cols - 2 * cdim), F32),
            jax.ShapeDtypeStruct((b * t, cdim), BF16),
            jax.ShapeDtypeStruct((b, hist, cdim), F32),
            jax.ShapeDtypeStruct(w_up.shape, BF16),
            jax.ShapeDtypeStruct(w_down.shape, BF16),
        ],
        scratch_shapes=[
            pltpu.VMEM((tm, d), BF16),
            pltpu.VMEM((CONV_HALO + tm, cdim), F32),
        ],
        compiler_params=_params("arbitrary", "arbitrary"),
        name="in_proj_conv",
    )(x3.reshape(b * t, d), g, w, state, w_dw, b_dw, ln_g, ln_b, w_up, w_down)


def _cumsum_rows(x, tri3):
    hi = x.astype(BF16)
    r1 = x - hi.astype(F32)
    mid = r1.astype(BF16)
    lo = (r1 - mid.astype(F32)).astype(BF16)
    return jnp.dot(tri3, jnp.concatenate([hi, mid, lo], axis=0), preferred_element_type=F32)


def _hgrn_kernel(q_ref, f_ref, i_ref, g_ref, s0_ref, lbl_ref, ng_ref, o_ref, sn_ref, st_ref,
                 *, tt, ch, blk, hb, layer):
    t = pl.program_id(2)
    d = HEAD_DIM
    nb = ch // blk
    half = blk // 2

    @pl.when(t == 0)
    def _():
        for h in range(hb):
            st_ref[h] = s0_ref[0, h].T

    lg = lbl_ref[...]
    e = jnp.exp(lg - jnp.max(lg, axis=0, keepdims=True))
    lbv = jnp.sum(e[0:layer + 1], axis=0, keepdims=True) / jnp.sum(e, axis=0, keepdims=True)
    f_off = 0.5 * (1.0 + lbv)
    f_amp = 0.5 * (1.0 - lbv)
    ngv = ng_ref[...]

    r_i = lax.broadcasted_iota(jnp.int32, (ch, ch), 0)
    c_i = lax.broadcasted_iota(jnp.int32, (ch, ch), 1)
    tri = r_i >= c_i
    tri_b = jnp.where(tri, 1.0, 0.0).astype(BF16)
    tri3 = jnp.concatenate([tri_b, tri_b, tri_b], axis=1)
    zeros_b = jnp.zeros((blk, d), BF16)

    def bcast_rows(vals):
        return jnp.concatenate([jnp.broadcast_to(m, (blk, m.shape[1])) for m in vals], axis=0)

    def decay(c):
        rows = pl.ds(c * ch, ch)
        f = f_off + f_amp * jnp.tanh(0.5 * f_ref[0, rows, :])
        return f, _cumsum_rows(jnp.log2(f), tri3)

    def scores(c, f, cum):
        rows = pl.ds(c * ch, ch)
        mids = [cum[j * blk + half - 1:j * blk + half, :] for j in range(nb)]
        last = cum[ch - 1:ch, :]
        mid_rows = bcast_rows(mids)
        from_start = bcast_rows([jnp.exp2(m) for m in mids])
        to_end = bcast_rows([jnp.exp2(last - m) for m in mids])
        g_end = jnp.exp2(last)
        qd = _silu(q_ref[0, rows, :]) * jnp.exp2(cum - mid_rows)
        kd = (1.0 - f) * jnp.exp2(mid_rows - cum)
        qd_b = qd.astype(BF16)
        kd_b = kd.astype(BF16)
        q_in = (qd * from_start).astype(BF16)
        k_end = (kd * to_end).astype(BF16)
        vb = i_ref[0, rows, :].astype(BF16)
        cross = {}
        for i in range(nb):
            for j in range(i):
                scale = jnp.broadcast_to(jnp.exp2(mids[i] - mids[j]), (blk, mids[i].shape[1]))
                cross[i, j] = (qd[i * blk:(i + 1) * blk] * scale).astype(BF16)
        atts = []
        for h in range(hb):
            ls = slice(h * d, (h + 1) * d)
            qcols, kcols = [], []
            for j in range(nb):
                qparts, kparts = [], []
                for i in range(nb):
                    rs = slice(i * blk, (i + 1) * blk)
                    if i < j:
                        qparts.append(zeros_b)
                    elif i == j:
                        qparts.append(qd_b[rs, ls])
                    else:
                        qparts.append(cross[i, j][:, ls])
                    kparts.append(kd_b[rs, ls] if i == j else zeros_b)
                qcols.append(jnp.concatenate(qparts, axis=0))
                kcols.append(jnp.concatenate(kparts, axis=0))
            qcat = jnp.concatenate(qcols, axis=1)
            kcat = jnp.concatenate(kcols, axis=1)
            atts.append(lax.dot_general(qcat, kcat, (((1,), (1,)), ((), ())), preferred_element_type=F32))
        return atts, q_in, k_end, vb, g_end

    def output(c, atts, q_in, k_end, vb, g_end):
        rows = pl.ds(c * ch, ch)
        heads = [slice(h * d, (h + 1) * d) for h in range(hb)]
        states = [st_ref[h] for h in range(hb)]
        outs = []
        for h, ls in enumerate(heads):
            att = jnp.where(tri, atts[h], 0.0).astype(BF16)
            o = jnp.dot(att, vb[:, ls], preferred_element_type=F32)
            outs.append(o + lax.dot_general(q_in[:, ls], states[h].astype(BF16),
                                            (((1,), (1,)), ((), ())), preferred_element_type=F32))
        for h, ls in enumerate(heads):
            upd = lax.dot_general(vb[:, ls], k_end[:, ls], (((0,), (0,)), ((), ())),
                                  preferred_element_type=F32)
            st_ref[h] = states[h] * g_end[:, ls] + upd
        outs = [o * lax.rsqrt(jnp.mean(o * o, axis=-1, keepdims=True) + EPS) for o in outs]
        o_all = jnp.concatenate(outs, axis=1) if hb > 1 else outs[0]
        o_ref[0, rows, :] = (o_all * ngv * _silu(g_ref[0, rows, :])).astype(BF16)

    n_chunks = tt // ch
    staged = scores(0, *decay(0))
    for c in range(n_chunks):
        nxt = decay(c + 1) if c + 1 < n_chunks else None
        output(c, *staged)
        if nxt is not None:
            staged = scores(c + 1, *nxt)

    @pl.when(t == pl.num_programs(2) - 1)
    def _():
        for h in range(hb):
            sn_ref[0, h] = st_ref[h].T


def _hgrn_group(p3, s0, lb_logits, norm_g, col0, layer):
    b, t, _ = p3.shape
    heads = s0.shape[1]
    d = HEAD_DIM
    hdim = heads * d
    hb = heads
    tt = _pick(t, (512, 256, 128, 64, 32, 16))
    ch = min(tt, 64)
    blk = min(ch, 32)
    lw = hb * d
    qb, fb, ib, gb = [(col0 + g * hdim) // lw for g in range(4)]
    kern = functools.partial(_hgrn_kernel, tt=tt, ch=ch, blk=blk, hb=hb, layer=layer)

    def col_spec(cb):
        return pl.BlockSpec((1, tt, lw), lambda i, h, j: (i, j, cb + h))

    return pl.pallas_call(
        kern,
        grid=(b, heads // hb, t // tt),
        in_specs=[
            col_spec(qb), col_spec(fb), col_spec(ib), col_spec(gb),
            pl.BlockSpec((1, hb, d, d), lambda i, h, j: (i, h, 0, 0)),
            pl.BlockSpec((lb_logits.shape[0], lw), lambda i, h, j: (0, h)),
            pl.BlockSpec((1, lw), lambda i, h, j: (0, h)),
        ],
        out_specs=[
            pl.BlockSpec((1, tt, lw), lambda i, h, j: (i, j, h)),
            pl.BlockSpec((1, hb, d, d), lambda i, h, j: (i, h, 0, 0)),
        ],
        out_shape=[
            jax.ShapeDtypeStruct((b, t, hdim), BF16),
            jax.ShapeDtypeStruct(s0.shape, F32),
        ],
        scratch_shapes=[pltpu.VMEM((hb, d, d), F32)],
        compiler_params=_params("parallel", "parallel", "arbitrary"),
        name="hgrn_group",
    )(p3, p3, p3, p3, s0, lb_logits, norm_g)


def _out_proj_kernel(x_ref, a_ref, b_ref, wa_ref, wb_ref, o_ref):
    o_ref[...] = (x_ref[...]
                  + jnp.dot(a_ref[...], wa_ref[...], preferred_element_type=F32)
                  + jnp.dot(b_ref[...], wb_ref[...], preferred_element_type=F32))


def _out_proj(x, a, bmix, w):
    n, d = x.shape
    ca, cb = a.shape[1], bmix.shape[1]
    assert ca == cb
    tm = _pick(n, (512, 256))
    return pl.pallas_call(
        _out_proj_kernel,
        grid=(n // tm,),
        in_specs=[
            pl.BlockSpec((tm, d), lambda i: (i, 0)),
            pl.BlockSpec((tm, ca), lambda i: (i, 0)),
            pl.BlockSpec((tm, cb), lambda i: (i, 0)),
            pl.BlockSpec((ca, d), lambda i: (0, 0)),
            pl.BlockSpec((cb, d), lambda i: (1, 0)),
        ],
        out_specs=pl.BlockSpec((tm, d), lambda i: (i, 0)),
        out_shape=jax.ShapeDtypeStruct((n, d), F32),
        compiler_params=_params("parallel"),
        name="out_proj",
    )(x, a, bmix, w, w)


def _mlp_kernel(h_ref, g_ref, wu_ref, wd_ref, gf_ref, o_ref, m_ref):
    j = pl.program_id(1)

    @pl.when(j == 0)
    def _():
        h = h_ref[...]
        ms = jnp.mean(h * h, axis=-1, keepdims=True)
        m_ref[...] = (h * lax.rsqrt(ms + EPS) * g_ref[...]).astype(BF16)
        o_ref[...] = h

    u = jnp.dot(m_ref[...], wu_ref[...], preferred_element_type=F32)
    r = jnp.square(jnp.maximum(u, 0.0)).astype(BF16)
    o_ref[...] += jnp.dot(r, wd_ref[...], preferred_element_type=F32)

    @pl.when(j == pl.num_programs(1) - 1)
    def _():
        y = o_ref[...]
        ms = jnp.mean(y * y, axis=-1, keepdims=True)
        o_ref[...] = y * lax.rsqrt(ms + EPS) * gf_ref[...]


def _mlp(h, g, w_up, w_down, g_final):
    n, d = h.shape
    dff = w_up.shape[1]
    tm = _pick(n, (1024, 512, 256))
    tf = _pick(dff, (2048, 512, 256) if tm <= 256 else (1024, 256))
    return pl.pallas_call(
        _mlp_kernel,
        grid=(n // tm, dff // tf),
        in_specs=[
            pl.BlockSpec((tm, d), lambda i, j: (i, 0)),
            pl.BlockSpec((1, d), lambda i, j: (0, 0)),
            pl.BlockSpec((d, tf), lambda i, j: (0, j)),
            pl.BlockSpec((tf, d), lambda i, j: (j, 0)),
            pl.BlockSpec((1, d), lambda i, j: (0, 0)),
        ],
        out_specs=pl.BlockSpec((tm, d), lambda i, j: (i, 0)),
        out_shape=jax.ShapeDtypeStruct((n, d), F32),
        scratch_shapes=[pltpu.VMEM((tm, d), BF16)],
        compiler_params=_params("parallel", "arbitrary"),
        name="mlp",
    )(h, g, w_up, w_down, g_final)


def _layer(x, conv_state, s0, lb_logits, layer, g_mix, w_in, w_dw, b_dw, ln_g, ln_b, hgrn_g, w_out,
           g_mlp, w_up, w_down, g_final):
    b, t, d = x.shape
    cdim = w_dw.shape[1]
    x2 = x.reshape(b * t, d)
    if t % FUSED_TIME_TILE == 0 and w_up.dtype == F32:
        p, a_out, new_conv, w_up, w_down = _in_proj_conv(
            x, g_mix, w_in, conv_state, w_dw, b_dw, ln_g, ln_b, w_up, w_down, FUSED_TIME_TILE)
        hgrn_col0 = 0
    else:
        w_up, w_down = w_up.astype(BF16), w_down.astype(BF16)
        p = _in_proj(x2, g_mix, w_in)
        a_out, new_conv = _conv_group(p.reshape(b, t, p.shape[1]), conv_state, w_dw, b_dw, ln_g, ln_b)
        hgrn_col0 = 2 * cdim
    b_out, new_s = _hgrn_group(p.reshape(b, t, p.shape[1]), s0, lb_logits, hgrn_g, hgrn_col0, layer)
    h = _out_proj(x2, a_out.reshape(b * t, -1), b_out.reshape(b * t, -1), w_out)
    y = _mlp(h, g_mlp, w_up, w_down, g_final)
    return y.reshape(b, t, d), new_conv, new_s, w_up, w_down


def kernel(x_prompt, x_sample, state_conv, state_hgrn, norm_mix_g, w_in, w_dw, b_dw, ln_conv_g, ln_conv_b,
           lb_logits, hgrn_norm_g, w_out, norm_mlp_g, w_up, w_down, norm_final_g):
    depth = w_in.shape[0]
    assert depth == 1, "single-layer trunk"
    l = 0
    bp = x_prompt.shape[0]
    mix_w = (norm_mix_g[l][None], w_in[l].astype(BF16), w_dw[l], b_dw[l][None], ln_conv_g[l][None],
             ln_conv_b[l][None], hgrn_norm_g[l][None], w_out[l].astype(BF16), norm_mlp_g[l][None])
    g_final = norm_final_g[None]
    zero_conv = jnp.zeros((bp,) + state_conv.shape[2:], state_conv.dtype)
    zero_s = jnp.zeros((bp,) + state_hgrn.shape[2:], state_hgrn.dtype)
    yp, cp, sp, w_up_b, w_down_b = _layer(x_prompt, zero_conv, zero_s, lb_logits, l, *mix_w,
                                          w_up[l], w_down[l], g_final)
    ys, cs, ss, _, _ = _layer(x_sample, state_conv[l], state_hgrn[l], lb_logits, l, *mix_w,
                              w_up_b, w_down_b, g_final)
    return (yp, ys, cp[None], sp[None], cs[None], ss[None])
```

```python
import functools

import jax
import jax.numpy as jnp
from jax import lax
from jax.experimental import pallas as pl
from jax.experimental.pallas import tpu as pltpu

EPS = 1e-6
HEAD_DIM = 128
CONV_HALO = 32
BF16_ROWS = 16
SINGLE_ROW_TILE = 256
FUSED_TIME_TILE = 1024
V7X_VMEM_LIMIT = 60 * 1024 * 1024

F32 = jnp.float32
BF16 = jnp.bfloat16


def _pick(n, candidates):
    for c in candidates:
        if n % c == 0:
            return c
    return n


def _params(*sem):
    return pltpu.CompilerParams(dimension_semantics=sem, vmem_limit_bytes=V7X_VMEM_LIMIT)


def _in_proj_kernel(x_ref, g_ref, w_ref, o_ref, n_ref):
    @pl.when(pl.program_id(1) == 0)
    def _():
        x = x_ref[...]
        ms = jnp.mean(x * x, axis=-1, keepdims=True)
        n_ref[...] = (x * lax.rsqrt(ms + EPS) * g_ref[...]).astype(BF16)

    o_ref[...] = jnp.dot(n_ref[...], w_ref[...], preferred_element_type=F32)


def _in_proj(x, g, w):
    n, d = x.shape
    cols = w.shape[1]
    tm = _pick(n, (1024, 512, 256))
    tn = _pick(cols, (3072, 1024, 512) if tm <= 256 else (1024, 512))
    return pl.pallas_call(
        _in_proj_kernel,
        grid=(n // tm, cols // tn),
        in_specs=[
            pl.BlockSpec((tm, d), lambda i, j: (i, 0)),
            pl.BlockSpec((1, d), lambda i, j: (0, 0)),
            pl.BlockSpec((d, tn), lambda i, j: (0, j)),
        ],
        out_specs=pl.BlockSpec((tm, tn), lambda i, j: (i, j)),
        out_shape=jax.ShapeDtypeStruct((n, cols), F32),
        scratch_shapes=[pltpu.VMEM((tm, d), BF16)],
        compiler_params=_params("parallel", "arbitrary"),
        name="in_proj",
    )(x, g, w)


def _in_proj_cast_kernel(x_ref, g_ref, w_ref, o_ref, wb_ref, n_ref):
    @pl.when(pl.program_id(0) == 0)
    def _():
        x = x_ref[...]
        ms = jnp.mean(x * x, axis=-1, keepdims=True)
        n_ref[...] = (x * lax.rsqrt(ms + EPS) * g_ref[...]).astype(BF16)

    wb = w_ref[...].astype(BF16)
    wb_ref[...] = wb
    o_ref[...] = jnp.dot(n_ref[...], wb, preferred_element_type=F32)


def _in_proj_cast(x, g, w):
    n, d = x.shape
    cols = w.shape[1]
    tn = _pick(cols, (1024, 512))
    return pl.pallas_call(
        _in_proj_cast_kernel,
        grid=(cols // tn,),
        in_specs=[
            pl.BlockSpec((n, d), lambda j: (0, 0)),
            pl.BlockSpec((1, d), lambda j: (0, 0)),
            pl.BlockSpec((d, tn), lambda j: (0, j)),
        ],
        out_specs=[
            pl.BlockSpec((n, tn), lambda j: (0, j)),
            pl.BlockSpec((d, tn), lambda j: (0, j)),
        ],
        out_shape=[
            jax.ShapeDtypeStruct((n, cols), F32),
            jax.ShapeDtypeStruct((d, cols), BF16),
        ],
        scratch_shapes=[pltpu.VMEM((n, d), BF16)],
        compiler_params=_params("arbitrary"),
        name="in_proj_cast",
    )(x, g, w)


def _sigmoid(x):
    return 0.5 + 0.5 * jnp.tanh(0.5 * x)


def _silu(x):
    h = 0.5 * x
    return h + h * jnp.tanh(h)


def _conv_rows(ubuf, base, rc, w_ref, b_ref, lg_ref, lb_ref, cdim, cw):
    hist = cw - 1
    lanes = HEAD_DIM
    pieces = []
    for lb in range(cdim // lanes):
        ls = slice(lb * lanes, (lb + 1) * lanes)
        acc = jnp.broadcast_to(b_ref[0:1, ls], (rc, lanes))
        win = ubuf[pl.ds(base, rc + CONV_HALO), ls]
        for res in range(8):
            offs = [o for o in range(CONV_HALO - hist, CONV_HALO + 1) if o % 8 == res]
            if not offs:
                continue
            sh = pltpu.roll(win, rc + CONV_HALO - res, axis=0) if res else win
            for o in offs:
                k = o - (CONV_HALO - hist)
                acc = acc + sh[o - res:o - res + rc] * w_ref[k:k + 1, ls]
        pieces.append(acc)
    c = jnp.concatenate(pieces, axis=-1)
    mu = jnp.mean(c, axis=-1, keepdims=True)
    cc = c - mu
    var = jnp.mean(cc * cc, axis=-1, keepdims=True)
    cn = cc * lax.rsqrt(var + EPS) * lg_ref[...] + lb_ref[...]
    return _silu(cn).astype(BF16)


def _conv_kernel(p_ref, st_ref, w_ref, b_ref, lg_ref, lb_ref, a_ref, ns_ref, ubuf, *, tt, rc, cdim, cw):
    t = pl.program_id(1)
    hist = cw - 1

    @pl.when(t == 0)
    def _():
        ubuf[0:CONV_HALO, :] = jnp.zeros((CONV_HALO, cdim), F32)
        ubuf[CONV_HALO - hist:CONV_HALO, :] = st_ref[0]

    @pl.when(t > 0)
    def _():
        ubuf[0:CONV_HALO, :] = ubuf[tt:tt + CONV_HALO, :]

    def chunk(i, carry):
        base = pl.multiple_of(i * rc, rc)
        a = p_ref[0, pl.ds(base, rc), 0:cdim]
        gate = p_ref[0, pl.ds(base, rc), cdim:2 * cdim]
        ubuf[pl.ds(CONV_HALO + base, rc), :] = a * _sigmoid(gate)
        a_ref[0, pl.ds(base, rc), :] = _conv_rows(ubuf, base, rc, w_ref, b_ref, lg_ref, lb_ref, cdim, cw)
        return carry

    lax.fori_loop(0, tt // rc, chunk, 0)

    @pl.when(t == pl.num_programs(1) - 1)
    def _():
        ns_ref[0] = ubuf[tt + CONV_HALO - hist:tt + CONV_HALO, :]


def _conv_group(p3, state, w_dw, b_dw, ln_g, ln_b, layer):
    b, t, _ = p3.shape
    _, cw, cdim = w_dw.shape
    hist = cw - 1
    assert hist <= CONV_HALO
    tt = _pick(t, (256, 128, 64, 32, 16))
    rc = min(tt, 64)
    kern = functools.partial(_conv_kernel, tt=tt, rc=rc, cdim=cdim, cw=cw)
    return pl.pallas_call(
        kern,
        grid=(b, t // tt),
        in_specs=[
            pl.BlockSpec((1, tt, 2 * cdim), lambda i, j: (i, j, 0)),
            pl.BlockSpec((None, 1, hist, cdim), lambda i, j: (layer, i, 0, 0)),
            pl.BlockSpec((None, cw, cdim), lambda i, j: (layer, 0, 0)),
            pl.BlockSpec((1, cdim), lambda i, j: (0, 0)),
            pl.BlockSpec((1, cdim), lambda i, j: (0, 0)),
            pl.BlockSpec((1, cdim), lambda i, j: (0, 0)),
        ],
        out_specs=[
            pl.BlockSpec((1, tt, cdim), lambda i, j: (i, j, 0)),
            pl.BlockSpec((None, 1, hist, cdim), lambda i, j: (0, i, 0, 0)),
        ],
        out_shape=[
            jax.ShapeDtypeStruct((b, t, cdim), BF16),
            jax.ShapeDtypeStruct((1, b, hist, cdim), F32),
        ],
        scratch_shapes=[pltpu.VMEM((CONV_HALO + max(tt, CONV_HALO), cdim), F32)],
        compiler_params=_params("parallel", "arbitrary"),
        name="conv_group",
    )(p3, state, w_dw, b_dw, ln_g, ln_b)


def _in_proj_conv_kernel(x_ref, g_ref, w_ref, st_ref, wdw_ref, bdw_ref, lg_ref, lb_ref, wo_ref, wu_ref, wd_ref,
                         p_ref, a_ref, ns_ref, wob_ref, wub_ref, wdb_ref, n_ref, ubuf,
                         *, tm, tps, rc, cdim, cw, n_cast):
    j = pl.program_id(1)
    t = pl.program_id(0) % tps
    hist = cw - 1
    n_conv_steps = pl.num_programs(1) - 2
    rows_per_step = tm // 4

    @pl.when(pl.program_id(0) * pl.num_programs(1) + j < n_cast)
    def _():
        wob_ref[...] = wo_ref[...].astype(BF16)
        wub_ref[...] = wu_ref[...].astype(BF16)
        wdb_ref[...] = wd_ref[...].astype(BF16)

    @pl.when(j == 0)
    def _():
        @pl.when(t == 0)
        def _():
            ubuf[0:CONV_HALO, :] = jnp.zeros((CONV_HALO, cdim), F32)
            ubuf[CONV_HALO - hist:CONV_HALO, :] = st_ref[0]

        @pl.when(t > 0)
        def _():
            ubuf[0:CONV_HALO, :] = ubuf[tm:tm + CONV_HALO, :]

        x = x_ref[...]
        ms = jnp.mean(x * x, axis=-1, keepdims=True)
        n_ref[...] = (x * lax.rsqrt(ms + EPS) * g_ref[...]).astype(BF16)
        ubuf[CONV_HALO:CONV_HALO + tm, :] = _sigmoid(jnp.dot(n_ref[...], w_ref[...], preferred_element_type=F32))

    @pl.when(j == 1)
    def _():
        ubuf[CONV_HALO:CONV_HALO + tm, :] = (
            jnp.dot(n_ref[...], w_ref[...], preferred_element_type=F32) * ubuf[CONV_HALO:CONV_HALO + tm, :])

    @pl.when(j >= 2)
    def _():
        p_ref[...] = jnp.dot(n_ref[...], w_ref[...], preferred_element_type=F32)
        for c in range(rows_per_step // rc):
            base = pl.multiple_of((j - 2) * rows_per_step + c * rc, rc)
            a_ref[pl.ds(base, rc), :] = _conv_rows(ubuf, base, rc, wdw_ref, bdw_ref, lg_ref, lb_ref, cdim, cw)

        @pl.when((j == n_conv_steps + 1) & (t == tps - 1))
        def _():
            ns_ref[0] = ubuf[tm + CONV_HALO - hist:tm + CONV_HALO, :]


def _in_proj_conv(x3, g, w, state, w_dw, b_dw, ln_g, ln_b, w_out, w_up, w_down, layer, tm):
    b, t, d = x3.shape
    _, cw, cdim = w_dw.shape
    hist = cw - 1
    cols = w.shape[1]
    tn = cdim
    nj = cols // tn
    assert hist <= CONV_HALO and t % tm == 0 and nj == 6 and tm % 256 == 0
    tps = t // tm
    n_cast = max(c for c in (64, 32, 16, 8, 4, 2, 1) if c <= b * tps * nj)
    cast_block = lambda i, j: (jnp.minimum(i * nj + j, n_cast - 1), 0)
    riders = (w_out, w_up, w_down)
    rider_rows = [r.shape[0] // n_cast for r in riders]
    assert all(rows * n_cast == r.shape[0] and rows % BF16_ROWS == 0 for rows, r in zip(rider_rows, riders))
    rider_specs = [pl.BlockSpec((rows, r.shape[1]), cast_block) for rows, r in zip(rider_rows, riders)]
    kern = functools.partial(_in_proj_conv_kernel, tm=tm, tps=tps, rc=64, cdim=cdim, cw=cw, n_cast=n_cast)
    const = lambda i, j: (0, 0)
    return pl.pallas_call(
        kern,
        grid=(b * tps, nj),
        in_specs=[
            pl.BlockSpec((tm, d), lambda i, j: (i, 0)),
            pl.BlockSpec((1, d), const),
            pl.BlockSpec((d, tn), lambda i, j: (0, jnp.where(j < 2, 1 - j, j))),
            pl.BlockSpec((None, 1, hist, cdim), lambda i, j: (layer, i // tps, 0, 0)),
            pl.BlockSpec((None, cw, cdim), lambda i, j: (layer, 0, 0)),
            pl.BlockSpec((1, cdim), const),
            pl.BlockSpec((1, cdim), const),
            pl.BlockSpec((1, cdim), const),
            *rider_specs,
        ],
        out_specs=[
            pl.BlockSpec((tm, tn), lambda i, j: (i, jnp.maximum(j - 2, 0))),
            pl.BlockSpec((tm, cdim), lambda i, j: (i, 0)),
            pl.BlockSpec((None, 1, hist, cdim), lambda i, j: (0, i // tps, 0, 0)),
            *rider_specs,
        ],
        out_shape=[
            jax.ShapeDtypeStruct((b * t, cols - 2 * cdim), F32),
            jax.ShapeDtypeStruct((b * t, cdim), BF16),
            jax.ShapeDtypeStruct((1, b, hist, cdim), F32),
            *[jax.ShapeDtypeStruct(r.shape, BF16) for r in riders],
        ],
        scratch_shapes=[
            pltpu.VMEM((tm, d), BF16),
            pltpu.VMEM((CONV_HALO + tm, cdim), F32),
        ],
        compiler_params=_params("arbitrary", "arbitrary"),
        name="in_proj_conv",
    )(x3.reshape(b * t, d), g, w, state, w_dw, b_dw, ln_g, ln_b, *riders)


def _cumsum_rows(x, tri3):
    hi = x.astype(BF16)
    r1 = x - hi.astype(F32)
    mid = r1.astype(BF16)
    lo = (r1 - mid.astype(F32)).astype(BF16)
    return jnp.dot(tri3, jnp.concatenate([hi, mid, lo], axis=0), preferred_element_type=F32)


def _hgrn_kernel(q_ref, f_ref, i_ref, g_ref, s0_ref, lbl_ref, ng_ref, o_ref, sn_ref, st_ref,
                 *, tt, ch, blk, hb, layer):
    t = pl.program_id(2)
    d = HEAD_DIM
    nb = ch // blk
    half = blk // 2

    @pl.when(t == 0)
    def _():
        for h in range(hb):
            st_ref[h] = s0_ref[0, h].T

    lg = lbl_ref[...]
    e = jnp.exp(lg - jnp.max(lg, axis=0, keepdims=True))
    lbv = jnp.sum(e[0:layer + 1], axis=0, keepdims=True) / jnp.sum(e, axis=0, keepdims=True)
    f_off = 0.5 * (1.0 + lbv)
    f_amp = 0.5 * (1.0 - lbv)
    ngv = ng_ref[...]

    r_i = lax.broadcasted_iota(jnp.int32, (ch, ch), 0)
    c_i = lax.broadcasted_iota(jnp.int32, (ch, ch), 1)
    tri = r_i >= c_i
    tri_b = jnp.where(tri, 1.0, 0.0).astype(BF16)
    tri3 = jnp.concatenate([tri_b, tri_b, tri_b], axis=1)
    zeros_b = jnp.zeros((blk, d), BF16)

    def bcast_rows(vals):
        return jnp.concatenate([jnp.broadcast_to(m, (blk, m.shape[1])) for m in vals], axis=0)

    def decay(c):
        rows = pl.ds(c * ch, ch)
        f = f_off + f_amp * jnp.tanh(0.5 * f_ref[0, rows, :])
        return f, _cumsum_rows(jnp.log2(f), tri3)

    def scores(c, f, cum):
        rows = pl.ds(c * ch, ch)
        mids = [cum[j * blk + half - 1:j * blk + half, :] for j in range(nb)]
        last = cum[ch - 1:ch, :]
        mid_rows = bcast_rows(mids)
        from_start = bcast_rows([jnp.exp2(m) for m in mids])
        to_end = bcast_rows([jnp.exp2(last - m) for m in mids])
        g_end = jnp.exp2(last)
        qd = _silu(q_ref[0, rows, :]) * jnp.exp2(cum - mid_rows)
        kd = (1.0 - f) * jnp.exp2(mid_rows - cum)
        qd_b = qd.astype(BF16)
        kd_b = kd.astype(BF16)
        q_in = (qd * from_start).astype(BF16)
        k_end = (kd * to_end).astype(BF16)
        vb = i_ref[0, rows, :].astype(BF16)
        cross = {}
        for i in range(nb):
            for j in range(i):
                scale = jnp.broadcast_to(jnp.exp2(mids[i] - mids[j]), (blk, mids[i].shape[1]))
                cross[i, j] = (qd[i * blk:(i + 1) * blk] * scale).astype(BF16)
        atts = []
        for h in range(hb):
            ls = slice(h * d, (h + 1) * d)
            qcols, kcols = [], []
            for j in range(nb):
                qparts, kparts = [], []
                for i in range(nb):
                    rs = slice(i * blk, (i + 1) * blk)
                    if i < j:
                        qparts.append(zeros_b)
                    elif i == j:
                        qparts.append(qd_b[rs, ls])
                    else:
                        qparts.append(cross[i, j][:, ls])
                    kparts.append(kd_b[rs, ls] if i == j else zeros_b)
                qcols.append(jnp.concatenate(qparts, axis=0))
                kcols.append(jnp.concatenate(kparts, axis=0))
            qcat = jnp.concatenate(qcols, axis=1)
            kcat = jnp.concatenate(kcols, axis=1)
            atts.append(lax.dot_general(qcat, kcat, (((1,), (1,)), ((), ())), preferred_element_type=F32))
        return atts, q_in, k_end, vb, g_end

    def output(c, atts, q_in, k_end, vb, g_end):
        rows = pl.ds(c * ch, ch)
        heads = [slice(h * d, (h + 1) * d) for h in range(hb)]
        states = [st_ref[h] for h in range(hb)]
        outs = []
        for h, ls in enumerate(heads):
            att = jnp.where(tri, atts[h], 0.0).astype(BF16)
            o = jnp.dot(att, vb[:, ls], preferred_element_type=F32)
            outs.append(o + lax.dot_general(q_in[:, ls], states[h].astype(BF16),
                                            (((1,), (1,)), ((), ())), preferred_element_type=F32))
        for h, ls in enumerate(heads):
            upd = lax.dot_general(vb[:, ls], k_end[:, ls], (((0,), (0,)), ((), ())),
                                  preferred_element_type=F32)
            st_ref[h] = states[h] * g_end[:, ls] + upd
        outs = [o * lax.rsqrt(jnp.mean(o * o, axis=-1, keepdims=True) + EPS) for o in outs]
        o_all = jnp.concatenate(outs, axis=1) if hb > 1 else outs[0]
        o_ref[0, rows, :] = (o_all * ngv * _silu(g_ref[0, rows, :])).astype(BF16)

    n_chunks = tt // ch
    staged = scores(0, *decay(0))
    for c in range(n_chunks):
        nxt = decay(c + 1) if c + 1 < n_chunks else None
        output(c, *staged)
        if nxt is not None:
            staged = scores(c + 1, *nxt)

    @pl.when(t == pl.num_programs(2) - 1)
    def _():
        for h in range(hb):
            sn_ref[0, h] = st_ref[h].T


def _hgrn_group(p3, s0, lb_logits, norm_g, col0, layer):
    b, t, _ = p3.shape
    heads = s0.shape[1]
    d = HEAD_DIM
    hdim = heads * d
    hb = heads
    tt = _pick(t, (512, 256, 128, 64, 32, 16))
    ch = min(tt, 64)
    blk = min(ch, 32)
    lw = hb * d
    qb, fb, ib, gb = [(col0 + g * hdim) // lw for g in range(4)]
    kern = functools.partial(_hgrn_kernel, tt=tt, ch=ch, blk=blk, hb=hb, layer=layer)

    def col_spec(cb):
        return pl.BlockSpec((1, tt, lw), lambda i, h, j: (i, j, cb + h))

    return pl.pallas_call(
        kern,
        grid=(b, heads // hb, t // tt),
        in_specs=[
            col_spec(qb), col_spec(fb), col_spec(ib), col_spec(gb),
            pl.BlockSpec((1, hb, d, d), lambda i, h, j: (i, h, 0, 0)),
            pl.BlockSpec((lb_logits.shape[0], lw), lambda i, h, j: (0, h)),
            pl.BlockSpec((1, lw), lambda i, h, j: (0, h)),
        ],
        out_specs=[
            pl.BlockSpec((1, tt, lw), lambda i, h, j: (i, j, h)),
            pl.BlockSpec((1, hb, d, d), lambda i, h, j: (i, h, 0, 0)),
        ],
        out_shape=[
            jax.ShapeDtypeStruct((b, t, hdim), BF16),
            jax.ShapeDtypeStruct(s0.shape, F32),
        ],
        scratch_shapes=[pltpu.VMEM((hb, d, d), F32)],
        compiler_params=_params("parallel", "parallel", "arbitrary"),
        name="hgrn_group",
    )(p3, p3, p3, p3, s0, lb_logits, norm_g)


def _out_proj_kernel(x_ref, a_ref, b_ref, wa_ref, wb_ref, o_ref):
    o_ref[...] = (x_ref[...]
                  + jnp.dot(a_ref[...], wa_ref[...], preferred_element_type=F32)
                  + jnp.dot(b_ref[...], wb_ref[...], preferred_element_type=F32))


def _out_proj(x, a, bmix, w):
    n, d = x.shape
    ca, cb = a.shape[1], bmix.shape[1]
    assert ca == cb
    tm = _pick(n, (512, 256))
    return pl.pallas_call(
        _out_proj_kernel,
        grid=(n // tm,),
        in_specs=[
            pl.BlockSpec((tm, d), lambda i: (i, 0)),
            pl.BlockSpec((tm, ca), lambda i: (i, 0)),
            pl.BlockSpec((tm, cb), lambda i: (i, 0)),
            pl.BlockSpec((ca, d), lambda i: (0, 0)),
            pl.BlockSpec((cb, d), lambda i: (1, 0)),
        ],
        out_specs=pl.BlockSpec((tm, d), lambda i: (i, 0)),
        out_shape=jax.ShapeDtypeStruct((n, d), F32),
        compiler_params=_params("parallel"),
        name="out_proj",
    )(x, a, bmix, w, w)


def _mlp_kernel(h_ref, g_ref, wu_ref, wd_ref, gf_ref, o_ref, m_ref):
    j = pl.program_id(1)

    @pl.when(j == 0)
    def _():
        h = h_ref[...]
        ms = jnp.mean(h * h, axis=-1, keepdims=True)
        m_ref[...] = (h * lax.rsqrt(ms + EPS) * g_ref[...]).astype(BF16)
        o_ref[...] = h

    u = jnp.dot(m_ref[...], wu_ref[...], preferred_element_type=F32)
    r = jnp.square(jnp.maximum(u, 0.0)).astype(BF16)
    o_ref[...] += jnp.dot(r, wd_ref[...], preferred_element_type=F32)

    @pl.when(j == pl.num_programs(1) - 1)
    def _():
        y = o_ref[...]
        ms = jnp.mean(y * y, axis=-1, keepdims=True)
        o_ref[...] = y * lax.rsqrt(ms + EPS) * gf_ref[...]


def _mlp(h, g, w_up, w_down, g_final):
    n, d = h.shape
    dff = w_up.shape[1]
    tm = _pick(n, (1024, 512, 256))
    tf = _pick(dff, (1024, 256))
    return pl.pallas_call(
        _mlp_kernel,
        grid=(n // tm, dff // tf),
        in_specs=[
            pl.BlockSpec((tm, d), lambda i, j: (i, 0)),
            pl.BlockSpec((1, d), lambda i, j: (0, 0)),
            pl.BlockSpec((d, tf), lambda i, j: (0, j)),
            pl.BlockSpec((tf, d), lambda i, j: (j, 0)),
            pl.BlockSpec((1, d), lambda i, j: (0, 0)),
        ],
        out_specs=pl.BlockSpec((tm, d), lambda i, j: (i, 0)),
        out_shape=jax.ShapeDtypeStruct((n, d), F32),
        scratch_shapes=[pltpu.VMEM((tm, d), BF16)],
        compiler_params=_params("parallel", "arbitrary"),
        name="mlp",
    )(h, g, w_up, w_down, g_final)


def _layer(x, conv_state, s0, lb_logits, layer, g_mix, w_in, w_dw, b_dw, ln_g, ln_b, hgrn_g, w_out,
           g_mlp, w_up, w_down, g_final, p=None):
    b, t, d = x.shape
    cdim = w_dw.shape[2]
    x2 = x.reshape(b * t, d)
    if p is None and t % FUSED_TIME_TILE == 0 and w_up.dtype == F32:
        p, a_out, new_conv, w_out, w_up, w_down = _in_proj_conv(
            x, g_mix, w_in, conv_state, w_dw, b_dw, ln_g, ln_b, w_out, w_up, w_down, layer, FUSED_TIME_TILE)
        hgrn_col0 = 0
    else:
        w_out, w_up, w_down = w_out.astype(BF16), w_up.astype(BF16), w_down.astype(BF16)
        if p is None:
            p = _in_proj(x2, g_mix, w_in)
        a_out, new_conv = _conv_group(p.reshape(b, t, p.shape[1]), conv_state, w_dw, b_dw, ln_g, ln_b, layer)
        hgrn_col0 = 2 * cdim
    b_out, new_s = _hgrn_group(p.reshape(b, t, p.shape[1]), s0, lb_logits, hgrn_g, hgrn_col0, layer)
    h = _out_proj(x2, a_out.reshape(b * t, -1), b_out.reshape(b * t, -1), w_out)
    y = _mlp(h, g_mlp, w_up, w_down, g_final)
    return y.reshape(b, t, d), new_conv, new_s, w_out, w_up, w_down


def kernel(x_prompt, x_sample, state_conv, state_hgrn, norm_mix_g, w_in, w_dw, b_dw, ln_conv_g, ln_conv_b,
           lb_logits, hgrn_norm_g, w_out, norm_mlp_g, w_up, w_down, norm_final_g):
    depth = w_in.shape[0]
    assert depth == 1, "single-layer trunk"
    l = 0
    bp = x_prompt.shape[0]
    g_mix = norm_mix_g[l][None]
    n_sample = x_sample.shape[0] * x_sample.shape[1]
    if n_sample <= SINGLE_ROW_TILE:
        p_sample, w_in_b = _in_proj_cast(x_sample.reshape(n_sample, -1), g_mix, w_in[l])
    else:
        p_sample, w_in_b = None, w_in[l].astype(BF16)
    mix_w = (g_mix, w_in_b, w_dw, b_dw[l][None], ln_conv_g[l][None], ln_conv_b[l][None], hgrn_norm_g[l][None])
    g_mlp, g_final = norm_mlp_g[l][None], norm_final_g[None]
    zero_conv = jnp.zeros((depth, bp) + state_conv.shape[2:], state_conv.dtype)
    zero_s = jnp.zeros((bp,) + state_hgrn.shape[2:], state_hgrn.dtype)
    yp, cp, sp, w_out_b, w_up_b, w_down_b = _layer(x_prompt, zero_conv, zero_s, lb_logits, l, *mix_w,
                                                   w_out[l], g_mlp, w_up[l], w_down[l], g_final)
    ys, cs, ss, _, _, _ = _layer(x_sample, state_conv, state_hgrn[l], lb_logits, l, *mix_w,
                                 w_out_b, g_mlp, w_up_b, w_down_b, g_final, p=p_sample)
    return (yp, ys, cp, sp[None], cs, ss[None])
```

```python
import functools

import jax
import jax.numpy as jnp
from jax import lax
from jax.experimental import pallas as pl
from jax.experimental.pallas import tpu as pltpu

EPS = 1e-6
HEAD_DIM = 128
CONV_HALO = 32
BF16_ROWS = 16
SINGLE_ROW_TILE = 256
FUSED_TIME_TILE = 1024
V7X_VMEM_LIMIT = 60 * 1024 * 1024

F32 = jnp.float32
BF16 = jnp.bfloat16


def _pick(n, candidates):
    for c in candidates:
        if n % c == 0:
            return c
    return n


def _params(*sem):
    return pltpu.CompilerParams(dimension_semantics=sem, vmem_limit_bytes=V7X_VMEM_LIMIT)


def _in_proj_kernel(x_ref, g_ref, w_ref, o_ref, n_ref):
    @pl.when(pl.program_id(1) == 0)
    def _():
        x = x_ref[...]
        ms = jnp.mean(x * x, axis=-1, keepdims=True)
        n_ref[...] = (x * lax.rsqrt(ms + EPS) * g_ref[...]).astype(BF16)

    o_ref[...] = jnp.dot(n_ref[...], w_ref[...], preferred_element_type=F32)


def _in_proj(x, g, w):
    n, d = x.shape
    cols = w.shape[1]
    tm = _pick(n, (1024, 512, 256))
    tn = _pick(cols, (3072, 1024, 512) if tm <= 256 else (1024, 512))
    return pl.pallas_call(
        _in_proj_kernel,
        grid=(n // tm, cols // tn),
        in_specs=[
            pl.BlockSpec((tm, d), lambda i, j: (i, 0)),
            pl.BlockSpec((1, d), lambda i, j: (0, 0)),
            pl.BlockSpec((d, tn), lambda i, j: (0, j)),
        ],
        out_specs=pl.BlockSpec((tm, tn), lambda i, j: (i, j)),
        out_shape=jax.ShapeDtypeStruct((n, cols), F32),
        scratch_shapes=[pltpu.VMEM((tm, d), BF16)],
        compiler_params=_params("parallel", "arbitrary"),
        name="in_proj",
    )(x, g, w)


def _in_proj_cast_kernel(x_ref, g_ref, w_ref, o_ref, wb_ref, n_ref):
    @pl.when(pl.program_id(0) == 0)
    def _():
        x = x_ref[...]
        ms = jnp.mean(x * x, axis=-1, keepdims=True)
        n_ref[...] = (x * lax.rsqrt(ms + EPS) * g_ref[...]).astype(BF16)

    wb = w_ref[...].astype(BF16)
    wb_ref[...] = wb
    o_ref[...] = jnp.dot(n_ref[...], wb, preferred_element_type=F32)


def _in_proj_cast(x, g, w):
    n, d = x.shape
    cols = w.shape[1]
    tn = _pick(cols, (1024, 512))
    return pl.pallas_call(
        _in_proj_cast_kernel,
        grid=(cols // tn,),
        in_specs=[
            pl.BlockSpec((n, d), lambda j: (0, 0)),
            pl.BlockSpec((1, d), lambda j: (0, 0)),
            pl.BlockSpec((d, tn), lambda j: (0, j)),
        ],
        out_specs=[
            pl.BlockSpec((n, tn), lambda j: (0, j)),
            pl.BlockSpec((d, tn), lambda j: (0, j)),
        ],
        out_shape=[
            jax.ShapeDtypeStruct((n, cols), F32),
            jax.ShapeDtypeStruct((d, cols), BF16),
        ],
        scratch_shapes=[pltpu.VMEM((n, d), BF16)],
        compiler_params=_params("arbitrary"),
        name="in_proj_cast",
    )(x, g, w)


def _sigmoid(x):
    return 0.5 + 0.5 * jnp.tanh(0.5 * x)


def _silu(x):
    h = 0.5 * x
    return h + h * jnp.tanh(h)


def _conv_rows(ubuf, base, rc, w_ref, b_ref, lg_ref, lb_ref, cdim, cw):
    hist = cw - 1
    lanes = HEAD_DIM
    pieces = []
    for lb in range(cdim // lanes):
        ls = slice(lb * lanes, (lb + 1) * lanes)
        acc = jnp.broadcast_to(b_ref[0:1, ls], (rc, lanes))
        win = ubuf[pl.ds(base, rc + CONV_HALO), ls]
        for res in range(8):
            offs = [o for o in range(CONV_HALO - hist, CONV_HALO + 1) if o % 8 == res]
            if not offs:
                continue
            sh = pltpu.roll(win, rc + CONV_HALO - res, axis=0) if res else win
            for o in offs:
                k = o - (CONV_HALO - hist)
                acc = acc + sh[o - res:o - res + rc] * w_ref[k:k + 1, ls]
        pieces.append(acc)
    c = jnp.concatenate(pieces, axis=-1)
    mu = jnp.mean(c, axis=-1, keepdims=True)
    cc = c - mu
    var = jnp.mean(cc * cc, axis=-1, keepdims=True)
    cn = cc * lax.rsqrt(var + EPS) * lg_ref[...] + lb_ref[...]
    return _silu(cn).astype(BF16)


def _conv_kernel(p_ref, st_ref, w_ref, b_ref, lg_ref, lb_ref, a_ref, ns_ref, ubuf, *, tt, rc, cdim, cw):
    t = pl.program_id(1)
    hist = cw - 1

    @pl.when(t == 0)
    def _():
        ubuf[0:CONV_HALO, :] = jnp.zeros((CONV_HALO, cdim), F32)
        ubuf[CONV_HALO - hist:CONV_HALO, :] = st_ref[0]

    @pl.when(t > 0)
    def _():
        ubuf[0:CONV_HALO, :] = ubuf[tt:tt + CONV_HALO, :]

    def chunk(i, carry):
        base = pl.multiple_of(i * rc, rc)
        a = p_ref[0, pl.ds(base, rc), 0:cdim]
        gate = p_ref[0, pl.ds(base, rc), cdim:2 * cdim]
        ubuf[pl.ds(CONV_HALO + base, rc), :] = a * _sigmoid(gate)
        a_ref[0, pl.ds(base, rc), :] = _conv_rows(ubuf, base, rc, w_ref, b_ref, lg_ref, lb_ref, cdim, cw)
        return carry

    lax.fori_loop(0, tt // rc, chunk, 0)

    @pl.when(t == pl.num_programs(1) - 1)
    def _():
        ns_ref[0] = ubuf[tt + CONV_HALO - hist:tt + CONV_HALO, :]


def _conv_group(p3, state, w_dw, b_dw, ln_g, ln_b, layer):
    b, t, _ = p3.shape
    _, cw, cdim = w_dw.shape
    hist = cw - 1
    assert hist <= CONV_HALO
    tt = _pick(t, (256, 128, 64, 32, 16))
    rc = min(tt, 64)
    kern = functools.partial(_conv_kernel, tt=tt, rc=rc, cdim=cdim, cw=cw)
    return pl.pallas_call(
        kern,
        grid=(b, t // tt),
        in_specs=[
            pl.BlockSpec((1, tt, 2 * cdim), lambda i, j: (i, j, 0)),
            pl.BlockSpec((None, 1, hist, cdim), lambda i, j: (layer, i, 0, 0)),
            pl.BlockSpec((None, cw, cdim), lambda i, j: (layer, 0, 0)),
            pl.BlockSpec((1, cdim), lambda i, j: (0, 0)),
            pl.BlockSpec((1, cdim), lambda i, j: (0, 0)),
            pl.BlockSpec((1, cdim), lambda i, j: (0, 0)),
        ],
        out_specs=[
            pl.BlockSpec((1, tt, cdim), lambda i, j: (i, j, 0)),
            pl.BlockSpec((None, 1, hist, cdim), lambda i, j: (0, i, 0, 0)),
        ],
        out_shape=[
            jax.ShapeDtypeStruct((b, t, cdim), BF16),
            jax.ShapeDtypeStruct((1, b, hist, cdim), F32),
        ],
        scratch_shapes=[pltpu.VMEM((CONV_HALO + max(tt, CONV_HALO), cdim), F32)],
        compiler_params=_params("parallel", "arbitrary"),
        name="conv_group",
    )(p3, state, w_dw, b_dw, ln_g, ln_b)


def _in_proj_conv_kernel(x_ref, g_ref, w_ref, st_ref, wdw_ref, bdw_ref, lg_ref, lb_ref, wo_ref, wu_ref, wd_ref,
                         p_ref, a_ref, ns_ref, wob_ref, wub_ref, wdb_ref, n_ref, ubuf,
                         *, tm, tps, rc, cdim, cw, n_cast):
    j = pl.program_id(1)
    t = pl.program_id(0) % tps
    hist = cw - 1
    n_conv_steps = pl.num_programs(1) - 2
    rows_per_step = tm // 4

    @pl.when(pl.program_id(0) * pl.num_programs(1) + j < n_cast)
    def _():
        wob_ref[...] = wo_ref[...].astype(BF16)
        wub_ref[...] = wu_ref[...].astype(BF16)
        wdb_ref[...] = wd_ref[...].astype(BF16)

    @pl.when(j == 0)
    def _():
        @pl.when(t == 0)
        def _():
            ubuf[0:CONV_HALO, :] = jnp.zeros((CONV_HALO, cdim), F32)
            ubuf[CONV_HALO - hist:CONV_HALO, :] = st_ref[0]

        @pl.when(t > 0)
        def _():
            ubuf[0:CONV_HALO, :] = ubuf[tm:tm + CONV_HALO, :]

        x = x_ref[...]
        ms = jnp.mean(x * x, axis=-1, keepdims=True)
        n_ref[...] = (x * lax.rsqrt(ms + EPS) * g_ref[...]).astype(BF16)
        ubuf[CONV_HALO:CONV_HALO + tm, :] = _sigmoid(jnp.dot(n_ref[...], w_ref[...], preferred_element_type=F32))

    @pl.when(j == 1)
    def _():
        ubuf[CONV_HALO:CONV_HALO + tm, :] = (
            jnp.dot(n_ref[...], w_ref[...], preferred_element_type=F32) * ubuf[CONV_HALO:CONV_HALO + tm, :])

    @pl.when(j >= 2)
    def _():
        p_ref[...] = jnp.dot(n_ref[...], w_ref[...], preferred_element_type=F32)
        for c in range(rows_per_step // rc):
            base = pl.multiple_of((j - 2) * rows_per_step + c * rc, rc)
            a_ref[pl.ds(base, rc), :] = _conv_rows(ubuf, base, rc, wdw_ref, bdw_ref, lg_ref, lb_ref, cdim, cw)

        @pl.when((j == n_conv_steps + 1) & (t == tps - 1))
        def _():
            ns_ref[0] = ubuf[tm + CONV_HALO - hist:tm + CONV_HALO, :]


def _in_proj_conv(x3, g, w, state, w_dw, b_dw, ln_g, ln_b, w_out, w_up, w_down, layer, tm):
    b, t, d = x3.shape
    _, cw, cdim = w_dw.shape
    hist = cw - 1
    cols = w.shape[1]
    tn = cdim
    nj = cols // tn
    assert hist <= CONV_HALO and t % tm == 0 and nj == 6 and tm % 256 == 0
    tps = t // tm
    n_cast = max(c for c in (64, 32, 16, 8, 4, 2, 1) if c <= b * tps * nj)
    cast_block = lambda i, j: (jnp.minimum(i * nj + j, n_cast - 1), 0)
    riders = (w_out, w_up, w_down)
    rider_rows = [r.shape[0] // n_cast for r in riders]
    assert all(rows * n_cast == r.shape[0] and rows % BF16_ROWS == 0 for rows, r in zip(rider_rows, riders))
    rider_specs = [pl.BlockSpec((rows, r.shape[1]), cast_block) for rows, r in zip(rider_rows, riders)]
    kern = functools.partial(_in_proj_conv_kernel, tm=tm, tps=tps, rc=64, cdim=cdim, cw=cw, n_cast=n_cast)
    const = lambda i, j: (0, 0)
    return pl.pallas_call(
        kern,
        grid=(b * tps, nj),
        in_specs=[
            pl.BlockSpec((tm, d), lambda i, j: (i, 0)),
            pl.BlockSpec((1, d), const),
            pl.BlockSpec((d, tn), lambda i, j: (0, jnp.where(j < 2, 1 - j, j))),
            pl.BlockSpec((None, 1, hist, cdim), lambda i, j: (layer, i // tps, 0, 0)),
            pl.BlockSpec((None, cw, cdim), lambda i, j: (layer, 0, 0)),
            pl.BlockSpec((1, cdim), const),
            pl.BlockSpec((1, cdim), const),
            pl.BlockSpec((1, cdim), const),
            *rider_specs,
        ],
        out_specs=[
            pl.BlockSpec((tm, tn), lambda i, j: (i, jnp.maximum(j - 2, 0))),
            pl.BlockSpec((tm, cdim), lambda i, j: (i, 0)),
            pl.BlockSpec((None, 1, hist, cdim), lambda i, j: (0, i // tps, 0, 0)),
            *rider_specs,
        ],
        out_shape=[
            jax.ShapeDtypeStruct((b * t, cols - 2 * cdim), F32),
            jax.ShapeDtypeStruct((b * t, cdim), BF16),
            jax.ShapeDtypeStruct((1, b, hist, cdim), F32),
            *[jax.ShapeDtypeStruct(r.shape, BF16) for r in riders],
        ],
        scratch_shapes=[
            pltpu.VMEM((tm, d), BF16),
            pltpu.VMEM((CONV_HALO + tm, cdim), F32),
        ],
        compiler_params=_params("arbitrary", "arbitrary"),
        name="in_proj_conv",
    )(x3.reshape(b * t, d), g, w, state, w_dw, b_dw, ln_g, ln_b, *riders)


def _cumsum_rows(x, tri3):
    hi = x.astype(BF16)
    r1 = x - hi.astype(F32)
    mid = r1.astype(BF16)
    lo = (r1 - mid.astype(F32)).astype(BF16)
    return jnp.dot(tri3, jnp.concatenate([hi, mid, lo], axis=0), preferred_element_type=F32)


def _hgrn_kernel(q_ref, f_ref, i_ref, g_ref, s0_ref, lbl_ref, ng_ref, o_ref, sn_ref, st_ref,
                 *, tt, ch, blk, hb, layer):
    t = pl.program_id(2)
    d = HEAD_DIM
    nb = ch // blk
    half = blk // 2

    @pl.when(t == 0)
    def _():
        for h in range(hb):
            st_ref[h] = s0_ref[0, h].T

    lg = lbl_ref[...]
    e = jnp.exp(lg - jnp.max(lg, axis=0, keepdims=True))
    lbv = jnp.sum(e[0:layer + 1], axis=0, keepdims=True) / jnp.sum(e, axis=0, keepdims=True)
    f_off = 0.5 * (1.0 + lbv)
    f_amp = 0.5 * (1.0 - lbv)
    ngv = ng_ref[...]

    r_i = lax.broadcasted_iota(jnp.int32, (ch, ch), 0)
    c_i = lax.broadcasted_iota(jnp.int32, (ch, ch), 1)
    tri = r_i >= c_i
    tri_b = jnp.where(tri, 1.0, 0.0).astype(BF16)
    tri3 = jnp.concatenate([tri_b, tri_b, tri_b], axis=1)
    zeros_b = jnp.zeros((blk, d), BF16)

    def bcast_rows(vals):
        return jnp.concatenate([jnp.broadcast_to(m, (blk, m.shape[1])) for m in vals], axis=0)

    def decay(c):
        rows = pl.ds(c * ch, ch)
        f = f_off + f_amp * jnp.tanh(0.5 * f_ref[0, rows, :])
        return f, _cumsum_rows(jnp.log2(f), tri3)

    def scores(c, f, cum):
        rows = pl.ds(c * ch, ch)
        mids = [cum[j * blk + half - 1:j * blk + half, :] for j in range(nb)]
        last = cum[ch - 1:ch, :]
        mid_rows = bcast_rows(mids)
        from_start = bcast_rows([jnp.exp2(m) for m in mids])
        to_end = bcast_rows([jnp.exp2(last - m) for m in mids])
        g_end = jnp.exp2(last)
        qd = _silu(q_ref[0, rows, :]) * jnp.exp2(cum - mid_rows)
        kd = (1.0 - f) * jnp.exp2(mid_rows - cum)
        qd_b = qd.astype(BF16)
        kd_b = kd.astype(BF16)
        q_in = (qd * from_start).astype(BF16)
        k_end = (kd * to_end).astype(BF16)
        vb = i_ref[0, rows, :].astype(BF16)
        cross = {}
        for i in range(nb):
            for j in range(i):
                scale = jnp.broadcast_to(jnp.exp2(mids[i] - mids[j]), (blk, mids[i].shape[1]))
                cross[i, j] = (qd[i * blk:(i + 1) * blk] * scale).astype(BF16)
        atts = []
        for h in range(hb):
            ls = slice(h * d, (h + 1) * d)
            qcols, kcols = [], []
            for j in range(nb):
                qparts, kparts = [], []
                for i in range(nb):
                    rs = slice(i * blk, (i + 1) * blk)
                    if i < j:
                        qparts.append(zeros_b)
                    elif i == j:
                        qparts.append(qd_b[rs, ls])
                    else:
                        qparts.append(cross[i, j][:, ls])
                    kparts.append(kd_b[rs, ls] if i == j else zeros_b)
                qcols.append(jnp.concatenate(qparts, axis=0))
                kcols.append(jnp.concatenate(kparts, axis=0))
            qcat = jnp.concatenate(qcols, axis=1)
            kcat = jnp.concatenate(kcols, axis=1)
            atts.append(lax.dot_general(qcat, kcat, (((1,), (1,)), ((), ())), preferred_element_type=F32))
        return atts, q_in, k_end, vb, g_end

    def output(c, atts, q_in, k_end, vb, g_end):
        rows = pl.ds(c * ch, ch)
        heads = [slice(h * d, (h + 1) * d) for h in range(hb)]
        states = [st_ref[h] for h in range(hb)]
        outs = []
        for h, ls in enumerate(heads):
            att = jnp.where(tri, atts[h], 0.0).astype(BF16)
            o = jnp.dot(att, vb[:, ls], preferred_element_type=F32)
            outs.append(o + lax.dot_general(q_in[:, ls], states[h].astype(BF16),
                                            (((1,), (1,)), ((), ())), preferred_element_type=F32))
        for h, ls in enumerate(heads):
            upd = lax.dot_general(vb[:, ls], k_end[:, ls], (((0,), (0,)), ((), ())),
                                  preferred_element_type=F32)
            st_ref[h] = states[h] * g_end[:, ls] + upd
        outs = [o * lax.rsqrt(jnp.mean(o * o, axis=-1, keepdims=True) + EPS) for o in outs]
        o_all = jnp.concatenate(outs, axis=1) if hb > 1 else outs[0]
        o_ref[0, rows, :] = (o_all * ngv * _silu(g_ref[0, rows, :])).astype(BF16)

    n_chunks = tt // ch
    staged = scores(0, *decay(0))
    for c in range(n_chunks):
        nxt = decay(c + 1) if c + 1 < n_chunks else None
        output(c, *staged)
        if nxt is not None:
            staged = scores(c + 1, *nxt)

    @pl.when(t == pl.num_programs(2) - 1)
    def _():
        for h in range(hb):
            sn_ref[0, h] = st_ref[h].T


def _hgrn_group(p3, s0, lb_logits, norm_g, col0, layer):
    b, t, _ = p3.shape
    heads = s0.shape[1]
    d = HEAD_DIM
    hdim = heads * d
    hb = heads
    tt = _pick(t, (1024, 512, 256, 128, 64, 32, 16))
    ch = min(tt, 64)
    blk = min(ch, 32)
    lw = hb * d
    qb, fb, ib, gb = [(col0 + g * hdim) // lw for g in range(4)]
    kern = functools.partial(_hgrn_kernel, tt=tt, ch=ch, blk=blk, hb=hb, layer=layer)

    def col_spec(cb):
        return pl.BlockSpec((1, tt, lw), lambda i, h, j: (i, j, cb + h))

    return pl.pallas_call(
        kern,
        grid=(b, heads // hb, t // tt),
        in_specs=[
            col_spec(qb), col_spec(fb), col_spec(ib), col_spec(gb),
            pl.BlockSpec((1, hb, d, d), lambda i, h, j: (i, h, 0, 0)),
            pl.BlockSpec((lb_logits.shape[0], lw), lambda i, h, j: (0, h)),
            pl.BlockSpec((1, lw), lambda i, h, j: (0, h)),
        ],
        out_specs=[
            pl.BlockSpec((1, tt, lw), lambda i, h, j: (i, j, h)),
            pl.BlockSpec((1, hb, d, d), lambda i, h, j: (i, h, 0, 0)),
        ],
        out_shape=[
            jax.ShapeDtypeStruct((b, t, hdim), BF16),
            jax.ShapeDtypeStruct(s0.shape, F32),
        ],
        scratch_shapes=[pltpu.VMEM((hb, d, d), F32)],
        compiler_params=_params("parallel", "parallel", "arbitrary"),
        name="hgrn_group",
    )(p3, p3, p3, p3, s0, lb_logits, norm_g)


def _out_proj_kernel(x_ref, a_ref, b_ref, wa_ref, wb_ref, o_ref):
    o_ref[...] = (x_ref[...]
                  + jnp.dot(a_ref[...], wa_ref[...], preferred_element_type=F32)
                  + jnp.dot(b_ref[...], wb_ref[...], preferred_element_type=F32))


def _out_proj(x, a, bmix, w):
    n, d = x.shape
    ca, cb = a.shape[1], bmix.shape[1]
    assert ca == cb
    tm = _pick(n, (1024, 512, 256))
    return pl.pallas_call(
        _out_proj_kernel,
        grid=(n // tm,),
        in_specs=[
            pl.BlockSpec((tm, d), lambda i: (i, 0)),
            pl.BlockSpec((tm, ca), lambda i: (i, 0)),
            pl.BlockSpec((tm, cb), lambda i: (i, 0)),
            pl.BlockSpec((ca, d), lambda i: (0, 0)),
            pl.BlockSpec((cb, d), lambda i: (1, 0)),
        ],
        out_specs=pl.BlockSpec((tm, d), lambda i: (i, 0)),
        out_shape=jax.ShapeDtypeStruct((n, d), F32),
        compiler_params=_params("parallel"),
        name="out_proj",
    )(x, a, bmix, w, w)


def _mlp_kernel(h_ref, g_ref, wu_ref, wd_ref, gf_ref, o_ref, m_ref):
    j = pl.program_id(1)

    @pl.when(j == 0)
    def _():
        h = h_ref[...]
        ms = jnp.mean(h * h, axis=-1, keepdims=True)
        m_ref[...] = (h * lax.rsqrt(ms + EPS) * g_ref[...]).astype(BF16)
        o_ref[...] = h

    u = jnp.dot(m_ref[...], wu_ref[...], preferred_element_type=F32)
    r = jnp.square(jnp.maximum(u, 0.0)).astype(BF16)
    o_ref[...] += jnp.dot(r, wd_ref[...], preferred_element_type=F32)

    @pl.when(j == pl.num_programs(1) - 1)
    def _():
        y = o_ref[...]
        ms = jnp.mean(y * y, axis=-1, keepdims=True)
        o_ref[...] = y * lax.rsqrt(ms + EPS) * gf_ref[...]


def _mlp(h, g, w_up, w_down, g_final):
    n, d = h.shape
    dff = w_up.shape[1]
    tm = _pick(n, (1024, 512, 256))
    tf = _pick(dff, (1024, 256))
    return pl.pallas_call(
        _mlp_kernel,
        grid=(n // tm, dff // tf),
        in_specs=[
            pl.BlockSpec((tm, d), lambda i, j: (i, 0)),
            pl.BlockSpec((1, d), lambda i, j: (0, 0)),
            pl.BlockSpec((d, tf), lambda i, j: (0, j)),
            pl.BlockSpec((tf, d), lambda i, j: (j, 0)),
            pl.BlockSpec((1, d), lambda i, j: (0, 0)),
        ],
        out_specs=pl.BlockSpec((tm, d), lambda i, j: (i, 0)),
        out_shape=jax.ShapeDtypeStruct((n, d), F32),
        scratch_shapes=[pltpu.VMEM((tm, d), BF16)],
        compiler_params=_params("parallel", "arbitrary"),
        name="mlp",
    )(h, g, w_up, w_down, g_final)


def _layer(x, conv_state, s0, lb_logits, layer, g_mix, w_in, w_dw, b_dw, ln_g, ln_b, hgrn_g, w_out,
           g_mlp, w_up, w_down, g_final, p=None):
    b, t, d = x.shape
    cdim = w_dw.shape[2]
    x2 = x.reshape(b * t, d)
    if p is None and t % FUSED_TIME_TILE == 0 and w_up.dtype == F32:
        p, a_out, new_conv, w_out, w_up, w_down = _in_proj_conv(
            x, g_mix, w_in, conv_state, w_dw, b_dw, ln_g, ln_b, w_out, w_up, w_down, layer, FUSED_TIME_TILE)
        hgrn_col0 = 0
    else:
        w_out, w_up, w_down = w_out.astype(BF16), w_up.astype(BF16), w_down.astype(BF16)
        if p is None:
            p = _in_proj(x2, g_mix, w_in)
        a_out, new_conv = _conv_group(p.reshape(b, t, p.shape[1]), conv_state, w_dw, b_dw, ln_g, ln_b, layer)
        hgrn_col0 = 2 * cdim
    b_out, new_s = _hgrn_group(p.reshape(b, t, p.shape[1]), s0, lb_logits, hgrn_g, hgrn_col0, layer)
    h = _out_proj(x2, a_out.reshape(b * t, -1), b_out.reshape(b * t, -1), w_out)
    y = _mlp(h, g_mlp, w_up, w_down, g_final)
    return y.reshape(b, t, d), new_conv, new_s, w_out, w_up, w_down


def kernel(x_prompt, x_sample, state_conv, state_hgrn, norm_mix_g, w_in, w_dw, b_dw, ln_conv_g, ln_conv_b,
           lb_logits, hgrn_norm_g, w_out, norm_mlp_g, w_up, w_down, norm_final_g):
    depth = w_in.shape[0]
    assert depth == 1, "single-layer trunk"
    l = 0
    bp = x_prompt.shape[0]
    g_mix = norm_mix_g[l][None]
    n_sample = x_sample.shape[0] * x_sample.shape[1]
    if n_sample <= SINGLE_ROW_TILE:
        p_sample, w_in_b = _in_proj_cast(x_sample.reshape(n_sample, -1), g_mix, w_in[l])
    else:
        p_sample, w_in_b = None, w_in[l].astype(BF16)
    mix_w = (g_mix, w_in_b, w_dw, b_dw[l][None], ln_conv_g[l][None], ln_conv_b[l][None], hgrn_norm_g[l][None])
    g_mlp, g_final = norm_mlp_g[l][None], norm_final_g[None]
    zero_conv = jnp.zeros((depth, bp) + state_conv.shape[2:], state_conv.dtype)
    zero_s = jnp.zeros((bp,) + state_hgrn.shape[2:], state_hgrn.dtype)
    yp, cp, sp, w_out_b, w_up_b, w_down_b = _layer(x_prompt, zero_conv, zero_s, lb_logits, l, *mix_w,
                                                   w_out[l], g_mlp, w_up[l], w_down[l], g_final)
    ys, cs, ss, _, _, _ = _layer(x_sample, state_conv, state_hgrn[l], lb_logits, l, *mix_w,
                                 w_out_b, g_mlp, w_up_b, w_down_b, g_final, p=p_sample)
    return (yp, ys, cp, sp[None], cs, ss[None])
```

```python
import functools

import jax
import jax.numpy as jnp
from jax import lax
from jax.experimental import pallas as pl
from jax.experimental.pallas import tpu as pltpu

EPS = 1e-6
HEAD_DIM = 128
CONV_HALO = 32
BF16_ROWS = 16
HGRN_INPUT_PIECES = 4
SINGLE_ROW_TILE = 256
FUSED_TIME_TILE = 1024
V7X_VMEM_LIMIT = 60 * 1024 * 1024

F32 = jnp.float32
BF16 = jnp.bfloat16


def _pick(n, candidates):
    for c in candidates:
        if n % c == 0:
            return c
    return n


def _params(*sem):
    return pltpu.CompilerParams(dimension_semantics=sem, vmem_limit_bytes=V7X_VMEM_LIMIT)


def _in_proj_kernel(x_ref, g_ref, w_ref, o_ref, n_ref):
    @pl.when(pl.program_id(1) == 0)
    def _():
        x = x_ref[...]
        ms = jnp.mean(x * x, axis=-1, keepdims=True)
        n_ref[...] = (x * lax.rsqrt(ms + EPS) * g_ref[...]).astype(BF16)

    o_ref[...] = jnp.dot(n_ref[...], w_ref[...], preferred_element_type=F32)


def _in_proj(x, g, w):
    n, d = x.shape
    cols = w.shape[1]
    tm = _pick(n, (1024, 512, 256))
    tn = _pick(cols, (3072, 1024, 512) if tm <= 256 else (1024, 512))
    return pl.pallas_call(
        _in_proj_kernel,
        grid=(n // tm, cols // tn),
        in_specs=[
            pl.BlockSpec((tm, d), lambda i, j: (i, 0)),
            pl.BlockSpec((1, d), lambda i, j: (0, 0)),
            pl.BlockSpec((d, tn), lambda i, j: (0, j)),
        ],
        out_specs=pl.BlockSpec((tm, tn), lambda i, j: (i, j)),
        out_shape=jax.ShapeDtypeStruct((n, cols), F32),
        scratch_shapes=[pltpu.VMEM((tm, d), BF16)],
        compiler_params=_params("parallel", "arbitrary"),
        name="in_proj",
    )(x, g, w)


def _in_proj_cast_kernel(x_ref, g_ref, w_ref, o_ref, wb_ref, n_ref):
    @pl.when(pl.program_id(0) == 0)
    def _():
        x = x_ref[...]
        ms = jnp.mean(x * x, axis=-1, keepdims=True)
        n_ref[...] = (x * lax.rsqrt(ms + EPS) * g_ref[...]).astype(BF16)

    wb = w_ref[...].astype(BF16)
    wb_ref[...] = wb
    o_ref[...] = jnp.dot(n_ref[...], wb, preferred_element_type=F32)


def _in_proj_cast(x, g, w):
    n, d = x.shape
    cols = w.shape[1]
    tn = _pick(cols, (1024, 512))
    return pl.pallas_call(
        _in_proj_cast_kernel,
        grid=(cols // tn,),
        in_specs=[
            pl.BlockSpec((n, d), lambda j: (0, 0)),
            pl.BlockSpec((1, d), lambda j: (0, 0)),
            pl.BlockSpec((d, tn), lambda j: (0, j)),
        ],
        out_specs=[
            pl.BlockSpec((n, tn), lambda j: (0, j)),
            pl.BlockSpec((d, tn), lambda j: (0, j)),
        ],
        out_shape=[
            jax.ShapeDtypeStruct((n, cols), F32),
            jax.ShapeDtypeStruct((d, cols), BF16),
        ],
        scratch_shapes=[pltpu.VMEM((n, d), BF16)],
        compiler_params=_params("arbitrary"),
        name="in_proj_cast",
    )(x, g, w)


def _sigmoid(x):
    return 0.5 + 0.5 * jnp.tanh(0.5 * x)


def _silu(x):
    h = 0.5 * x
    return h + h * jnp.tanh(h)


def _conv_rows(ubuf, base, rc, w_ref, b_ref, lg_ref, lb_ref, cdim, cw):
    hist = cw - 1
    lanes = HEAD_DIM
    pieces = []
    for lb in range(cdim // lanes):
        ls = slice(lb * lanes, (lb + 1) * lanes)
        acc = jnp.broadcast_to(b_ref[0:1, ls], (rc, lanes))
        win = ubuf[pl.ds(base, rc + CONV_HALO), ls]
        for res in range(8):
            offs = [o for o in range(CONV_HALO - hist, CONV_HALO + 1) if o % 8 == res]
            if not offs:
                continue
            sh = pltpu.roll(win, rc + CONV_HALO - res, axis=0) if res else win
            for o in offs:
                k = o - (CONV_HALO - hist)
                acc = acc + sh[o - res:o - res + rc] * w_ref[k:k + 1, ls]
        pieces.append(acc)
    c = jnp.concatenate(pieces, axis=-1)
    mu = jnp.mean(c, axis=-1, keepdims=True)
    cc = c - mu
    var = jnp.mean(cc * cc, axis=-1, keepdims=True)
    cn = cc * lax.rsqrt(var + EPS) * lg_ref[...] + lb_ref[...]
    return _silu(cn).astype(BF16)


def _conv_kernel(p_ref, st_ref, w_ref, b_ref, lg_ref, lb_ref, a_ref, ns_ref, ubuf, *, tt, rc, cdim, cw):
    t = pl.program_id(1)
    hist = cw - 1

    @pl.when(t == 0)
    def _():
        ubuf[0:CONV_HALO, :] = jnp.zeros((CONV_HALO, cdim), F32)
        ubuf[CONV_HALO - hist:CONV_HALO, :] = st_ref[0]

    @pl.when(t > 0)
    def _():
        ubuf[0:CONV_HALO, :] = ubuf[tt:tt + CONV_HALO, :]

    def chunk(i, carry):
        base = pl.multiple_of(i * rc, rc)
        a = p_ref[0, pl.ds(base, rc), 0:cdim]
        gate = p_ref[0, pl.ds(base, rc), cdim:2 * cdim]
        ubuf[pl.ds(CONV_HALO + base, rc), :] = a * _sigmoid(gate)
        a_ref[0, pl.ds(base, rc), :] = _conv_rows(ubuf, base, rc, w_ref, b_ref, lg_ref, lb_ref, cdim, cw)
        return carry

    lax.fori_loop(0, tt // rc, chunk, 0)

    @pl.when(t == pl.num_programs(1) - 1)
    def _():
        ns_ref[0] = ubuf[tt + CONV_HALO - hist:tt + CONV_HALO, :]


def _conv_group(p3, state, w_dw, b_dw, ln_g, ln_b, layer):
    b, t, _ = p3.shape
    _, cw, cdim = w_dw.shape
    hist = cw - 1
    assert hist <= CONV_HALO
    tt = _pick(t, (256, 128, 64, 32, 16))
    rc = min(tt, 64)
    kern = functools.partial(_conv_kernel, tt=tt, rc=rc, cdim=cdim, cw=cw)
    return pl.pallas_call(
        kern,
        grid=(b, t // tt),
        in_specs=[
            pl.BlockSpec((1, tt, 2 * cdim), lambda i, j: (i, j, 0)),
            pl.BlockSpec((None, 1, hist, cdim), lambda i, j: (layer, i, 0, 0)),
            pl.BlockSpec((None, cw, cdim), lambda i, j: (layer, 0, 0)),
            pl.BlockSpec((1, cdim), lambda i, j: (0, 0)),
            pl.BlockSpec((1, cdim), lambda i, j: (0, 0)),
            pl.BlockSpec((1, cdim), lambda i, j: (0, 0)),
        ],
        out_specs=[
            pl.BlockSpec((1, tt, cdim), lambda i, j: (i, j, 0)),
            pl.BlockSpec((None, 1, hist, cdim), lambda i, j: (0, i, 0, 0)),
        ],
        out_shape=[
            jax.ShapeDtypeStruct((b, t, cdim), BF16),
            jax.ShapeDtypeStruct((1, b, hist, cdim), F32),
        ],
        scratch_shapes=[pltpu.VMEM((CONV_HALO + max(tt, CONV_HALO), cdim), F32)],
        compiler_params=_params("parallel", "arbitrary"),
        name="conv_group",
    )(p3, state, w_dw, b_dw, ln_g, ln_b)


def _in_proj_conv_kernel(x_ref, g_ref, w_ref, st_ref, wdw_ref, bdw_ref, lg_ref, lb_ref, wo_ref, wu_ref, wd_ref,
                         p_ref, a_ref, ns_ref, wob_ref, wub_ref, wdb_ref, n_ref, ubuf,
                         *, tm, tps, rc, cdim, cw, n_cast):
    j = pl.program_id(1)
    t = pl.program_id(0) % tps
    hist = cw - 1
    n_conv_steps = pl.num_programs(1) - 2
    rows_per_step = tm // 4

    @pl.when(pl.program_id(0) * pl.num_programs(1) + j < n_cast)
    def _():
        wob_ref[...] = wo_ref[...].astype(BF16)
        wub_ref[...] = wu_ref[...].astype(BF16)
        wdb_ref[...] = wd_ref[...].astype(BF16)

    @pl.when(j == 0)
    def _():
        @pl.when(t == 0)
        def _():
            ubuf[0:CONV_HALO, :] = jnp.zeros((CONV_HALO, cdim), F32)
            ubuf[CONV_HALO - hist:CONV_HALO, :] = st_ref[0]

        @pl.when(t > 0)
        def _():
            ubuf[0:CONV_HALO, :] = ubuf[tm:tm + CONV_HALO, :]

        x = x_ref[...]
        ms = jnp.mean(x * x, axis=-1, keepdims=True)
        n_ref[...] = (x * lax.rsqrt(ms + EPS) * g_ref[...]).astype(BF16)
        ubuf[CONV_HALO:CONV_HALO + tm, :] = _sigmoid(jnp.dot(n_ref[...], w_ref[...], preferred_element_type=F32))

    @pl.when(j == 1)
    def _():
        ubuf[CONV_HALO:CONV_HALO + tm, :] = (
            jnp.dot(n_ref[...], w_ref[...], preferred_element_type=F32) * ubuf[CONV_HALO:CONV_HALO + tm, :])

    @pl.when(j >= 2)
    def _():
        p_ref[...] = jnp.dot(n_ref[...], w_ref[...], preferred_element_type=F32)
        for c in range(rows_per_step // rc):
            base = pl.multiple_of((j - 2) * rows_per_step + c * rc, rc)
            a_ref[pl.ds(base, rc), :] = _conv_rows(ubuf, base, rc, wdw_ref, bdw_ref, lg_ref, lb_ref, cdim, cw)

        @pl.when((j == n_conv_steps + 1) & (t == tps - 1))
        def _():
            ns_ref[0] = ubuf[tm + CONV_HALO - hist:tm + CONV_HALO, :]


def _in_proj_conv(x3, g, w, state, w_dw, b_dw, ln_g, ln_b, w_out, w_up, w_down, layer, tm):
    b, t, d = x3.shape
    _, cw, cdim = w_dw.shape
    hist = cw - 1
    cols = w.shape[1]
    tn = cdim
    nj = cols // tn
    assert hist <= CONV_HALO and t % tm == 0 and nj == 6 and tm % 256 == 0
    tps = t // tm
    n_cast = max(c for c in (64, 32, 16, 8, 4, 2, 1) if c <= b * tps * nj)
    cast_block = lambda i, j: (jnp.minimum(i * nj + j, n_cast - 1), 0)
    riders = (w_out, w_up, w_down)
    rider_rows = [r.shape[0] // n_cast for r in riders]
    assert all(rows * n_cast == r.shape[0] and rows % BF16_ROWS == 0 for rows, r in zip(rider_rows, riders))
    rider_specs = [pl.BlockSpec((rows, r.shape[1]), cast_block) for rows, r in zip(rider_rows, riders)]
    kern = functools.partial(_in_proj_conv_kernel, tm=tm, tps=tps, rc=64, cdim=cdim, cw=cw, n_cast=n_cast)
    const = lambda i, j: (0, 0)
    return pl.pallas_call(
        kern,
        grid=(b * tps, nj),
        in_specs=[
            pl.BlockSpec((tm, d), lambda i, j: (i, 0)),
            pl.BlockSpec((1, d), const),
            pl.BlockSpec((d, tn), lambda i, j: (0, jnp.where(j < 2, 1 - j, j))),
            pl.BlockSpec((None, 1, hist, cdim), lambda i, j: (layer, i // tps, 0, 0)),
            pl.BlockSpec((None, cw, cdim), lambda i, j: (layer, 0, 0)),
            pl.BlockSpec((1, cdim), const),
            pl.BlockSpec((1, cdim), const),
            pl.BlockSpec((1, cdim), const),
            *rider_specs,
        ],
        out_specs=[
            pl.BlockSpec((None, tm, tn), lambda i, j: (jnp.maximum(j - 2, 0), i, 0)),
            pl.BlockSpec((tm, cdim), lambda i, j: (i, 0)),
            pl.BlockSpec((None, 1, hist, cdim), lambda i, j: (0, i // tps, 0, 0)),
            *rider_specs,
        ],
        out_shape=[
            jax.ShapeDtypeStruct((nj - 2, b * t, tn), F32),
            jax.ShapeDtypeStruct((b * t, cdim), BF16),
            jax.ShapeDtypeStruct((1, b, hist, cdim), F32),
            *[jax.ShapeDtypeStruct(r.shape, BF16) for r in riders],
        ],
        scratch_shapes=[
            pltpu.VMEM((tm, d), BF16),
            pltpu.VMEM((CONV_HALO + tm, cdim), F32),
        ],
        compiler_params=_params("arbitrary", "arbitrary"),
        name="in_proj_conv",
    )(x3.reshape(b * t, d), g, w, state, w_dw, b_dw, ln_g, ln_b, *riders)


def _cumsum_rows(x, tri3):
    hi = x.astype(BF16)
    r1 = x - hi.astype(F32)
    mid = r1.astype(BF16)
    lo = (r1 - mid.astype(F32)).astype(BF16)
    return jnp.dot(tri3, jnp.concatenate([hi, mid, lo], axis=0), preferred_element_type=F32)


def _hgrn_kernel(*refs, tt, ch, blk, hb, layer, n_split):
    pieces = [refs[k * n_split:(k + 1) * n_split] for k in range(4)]
    s0_ref, lbl_ref, ng_ref, o_ref, sn_ref, st_ref = refs[4 * n_split:]
    piece_rows = tt // n_split

    def rows_of(group, c):
        start = c * ch
        return pieces[group][start // piece_rows][0, pl.ds(start % piece_rows, ch), :]

    t = pl.program_id(2)
    d = HEAD_DIM
    nb = ch // blk
    half = blk // 2

    @pl.when(t == 0)
    def _():
        for h in range(hb):
            st_ref[h] = s0_ref[0, h].T

    lg = lbl_ref[...]
    e = jnp.exp(lg - jnp.max(lg, axis=0, keepdims=True))
    lbv = jnp.sum(e[0:layer + 1], axis=0, keepdims=True) / jnp.sum(e, axis=0, keepdims=True)
    f_off = 0.5 * (1.0 + lbv)
    f_amp = 0.5 * (1.0 - lbv)
    ngv = ng_ref[...]

    r_i = lax.broadcasted_iota(jnp.int32, (ch, ch), 0)
    c_i = lax.broadcasted_iota(jnp.int32, (ch, ch), 1)
    tri = r_i >= c_i
    tri_b = jnp.where(tri, 1.0, 0.0).astype(BF16)
    tri3 = jnp.concatenate([tri_b, tri_b, tri_b], axis=1)
    zeros_b = jnp.zeros((blk, d), BF16)

    def bcast_rows(vals):
        return jnp.concatenate([jnp.broadcast_to(m, (blk, m.shape[1])) for m in vals], axis=0)

    def decay(c):
        f = lbv + (1.0 - lbv) * jax.nn.sigmoid(rows_of(1, c))
        return f, _cumsum_rows(jnp.log2(f), tri3)

    def scores(c, f, cum):
        mids = [cum[j * blk + half - 1:j * blk + half, :] for j in range(nb)]
        last = cum[ch - 1:ch, :]
        mid_rows = bcast_rows(mids)
        from_start = bcast_rows([jnp.exp2(m) for m in mids])
        to_end = bcast_rows([jnp.exp2(last - m) for m in mids])
        g_end = jnp.exp2(last)
        qp = rows_of(0, c)
        qd = qp * jax.nn.sigmoid(qp) * jnp.exp2(cum - mid_rows)
        kd = (1.0 - f) * jnp.exp2(mid_rows - cum)
        qd_b = qd.astype(BF16)
        kd_b = kd.astype(BF16)
        q_in = (qd * from_start).astype(BF16)
        k_end = (kd * to_end).astype(BF16)
        vb = rows_of(2, c).astype(BF16)
        cross = {}
        for i in range(nb):
            for j in range(i):
                scale = jnp.broadcast_to(jnp.exp2(mids[i] - mids[j]), (blk, mids[i].shape[1]))
                cross[i, j] = (qd[i * blk:(i + 1) * blk] * scale).astype(BF16)
        atts = []
        for h in range(hb):
            ls = slice(h * d, (h + 1) * d)
            qcols, kcols = [], []
            for j in range(nb):
                qparts, kparts = [], []
                for i in range(nb):
                    rs = slice(i * blk, (i + 1) * blk)
                    if i < j:
                        qparts.append(zeros_b)
                    elif i == j:
                        qparts.append(qd_b[rs, ls])
                    else:
                        qparts.append(cross[i, j][:, ls])
                    kparts.append(kd_b[rs, ls] if i == j else zeros_b)
                qcols.append(jnp.concatenate(qparts, axis=0))
                kcols.append(jnp.concatenate(kparts, axis=0))
            qcat = jnp.concatenate(qcols, axis=1)
            kcat = jnp.concatenate(kcols, axis=1)
            atts.append(lax.dot_general(qcat, kcat, (((1,), (1,)), ((), ())), preferred_element_type=F32))
        return atts, q_in, k_end, vb, g_end

    def output(c, atts, q_in, k_end, vb, g_end):
        rows = pl.ds(c * ch, ch)
        heads = [slice(h * d, (h + 1) * d) for h in range(hb)]
        states = [st_ref[h] for h in range(hb)]
        outs = []
        for h, ls in enumerate(heads):
            att = jnp.where(tri, atts[h], 0.0).astype(BF16)
            o = jnp.dot(att, vb[:, ls], preferred_element_type=F32)
            outs.append(o + lax.dot_general(q_in[:, ls], states[h].astype(BF16),
                                            (((1,), (1,)), ((), ())), preferred_element_type=F32))
        for h, ls in enumerate(heads):
            upd = lax.dot_general(vb[:, ls], k_end[:, ls], (((0,), (0,)), ((), ())),
                                  preferred_element_type=F32)
            st_ref[h] = states[h] * g_end[:, ls] + upd
        outs = [o * lax.rsqrt(jnp.mean(o * o, axis=-1, keepdims=True) + EPS) for o in outs]
        o_all = jnp.concatenate(outs, axis=1) if hb > 1 else outs[0]
        gp = rows_of(3, c)
        o_ref[0, rows, :] = (o_all * ngv * (gp * jax.nn.sigmoid(gp))).astype(BF16)

    n_chunks = tt // ch
    staged = scores(0, *decay(0))
    for c in range(n_chunks):
        nxt = decay(c + 1) if c + 1 < n_chunks else None
        output(c, *staged)
        if nxt is not None:
            staged = scores(c + 1, *nxt)

    @pl.when(t == pl.num_programs(2) - 1)
    def _():
        for h in range(hb):
            sn_ref[0, h] = st_ref[h].T


def _hgrn_group(p, s0, lb_logits, norm_g, col0, layer):
    heads = s0.shape[1]
    d = HEAD_DIM
    hdim = heads * d
    hb = heads
    lw = hb * d
    b, t = p.shape[-3], p.shape[-2]
    tt = _pick(t, (1024, 512, 256, 128, 64, 32, 16))
    ch = min(tt, 64)
    blk = min(ch, 32)
    n_split = HGRN_INPUT_PIECES if tt % (HGRN_INPUT_PIECES * ch) == 0 else 1
    rows = tt // n_split
    if col0 is None:
        in_spec = lambda g, s: pl.BlockSpec((None, 1, rows, lw), lambda i, h, j: (g, i, j * n_split + s, h))
    else:
        in_spec = lambda g, s: pl.BlockSpec(
            (1, rows, lw), lambda i, h, j: (i, j * n_split + s, (col0 + g * hdim) // lw + h))
    kern = functools.partial(_hgrn_kernel, tt=tt, ch=ch, blk=blk, hb=hb, layer=layer, n_split=n_split)

    return pl.pallas_call(
        kern,
        grid=(b, heads // hb, t // tt),
        in_specs=[
            *[in_spec(g, s) for g in range(4) for s in range(n_split)],
            pl.BlockSpec((1, hb, d, d), lambda i, h, j: (i, h, 0, 0)),
            pl.BlockSpec((lb_logits.shape[0], lw), lambda i, h, j: (0, h)),
            pl.BlockSpec((1, lw), lambda i, h, j: (0, h)),
        ],
        out_specs=[
            pl.BlockSpec((1, tt, lw), lambda i, h, j: (i, j, h)),
            pl.BlockSpec((1, hb, d, d), lambda i, h, j: (i, h, 0, 0)),
        ],
        out_shape=[
            jax.ShapeDtypeStruct((b, t, hdim), BF16),
            jax.ShapeDtypeStruct(s0.shape, F32),
        ],
        scratch_shapes=[pltpu.VMEM((hb, d, d), F32)],
        compiler_params=_params("parallel", "parallel", "arbitrary"),
        name="hgrn_group",
    )(*([p] * (4 * n_split)), s0, lb_logits, norm_g)


def _out_proj_kernel(x_ref, a_ref, b_ref, wa_ref, wb_ref, o_ref):
    o_ref[...] = (x_ref[...]
                  + jnp.dot(a_ref[...], wa_ref[...], preferred_element_type=F32)
                  + jnp.dot(b_ref[...], wb_ref[...], preferred_element_type=F32))


def _out_proj(x, a, bmix, w):
    n, d = x.shape
    ca, cb = a.shape[1], bmix.shape[1]
    assert ca == cb
    tm = _pick(n, (512, 256))
    return pl.pallas_call(
        _out_proj_kernel,
        grid=(n // tm,),
        in_specs=[
            pl.BlockSpec((tm, d), lambda i: (i, 0)),
            pl.BlockSpec((tm, ca), lambda i: (i, 0)),
            pl.BlockSpec((tm, cb), lambda i: (i, 0)),
            pl.BlockSpec((ca, d), lambda i: (0, 0)),
            pl.BlockSpec((cb, d), lambda i: (1, 0)),
        ],
        out_specs=pl.BlockSpec((tm, d), lambda i: (i, 0)),
        out_shape=jax.ShapeDtypeStruct((n, d), F32),
        compiler_params=_params("parallel"),
        name="out_proj",
    )(x, a, bmix, w, w)


def _mlp_kernel(h_ref, g_ref, wu_ref, wd_ref, gf_ref, o_ref, m_ref):
    j = pl.program_id(1)

    @pl.when(j == 0)
    def _():
        h = h_ref[...]
        ms = jnp.mean(h * h, axis=-1, keepdims=True)
        m_ref[...] = (h * lax.rsqrt(ms + EPS) * g_ref[...]).astype(BF16)
        o_ref[...] = h

    u = jnp.dot(m_ref[...], wu_ref[...], preferred_element_type=F32)
    r = jnp.square(jnp.maximum(u, 0.0)).astype(BF16)
    o_ref[...] += jnp.dot(r, wd_ref[...], preferred_element_type=F32)

    @pl.when(j == pl.num_programs(1) - 1)
    def _():
        y = o_ref[...]
        ms = jnp.mean(y * y, axis=-1, keepdims=True)
        o_ref[...] = y * lax.rsqrt(ms + EPS) * gf_ref[...]


def _mlp(h, g, w_up, w_down, g_final):
    n, d = h.shape
    dff = w_up.shape[1]
    tm = _pick(n, (1024, 512, 256))
    tf = _pick(dff, (1024, 256))
    return pl.pallas_call(
        _mlp_kernel,
        grid=(n // tm, dff // tf),
        in_specs=[
            pl.BlockSpec((tm, d), lambda i, j: (i, 0)),
            pl.BlockSpec((1, d), lambda i, j: (0, 0)),
            pl.BlockSpec((d, tf), lambda i, j: (0, j)),
            pl.BlockSpec((tf, d), lambda i, j: (j, 0)),
            pl.BlockSpec((1, d), lambda i, j: (0, 0)),
        ],
        out_specs=pl.BlockSpec((tm, d), lambda i, j: (i, 0)),
        out_shape=jax.ShapeDtypeStruct((n, d), F32),
        scratch_shapes=[pltpu.VMEM((tm, d), BF16)],
        compiler_params=_params("parallel", "arbitrary"),
        name="mlp",
    )(h, g, w_up, w_down, g_final)


def _layer(x, conv_state, s0, lb_logits, layer, g_mix, w_in, w_dw, b_dw, ln_g, ln_b, hgrn_g, w_out,
           g_mlp, w_up, w_down, g_final, p=None):
    b, t, d = x.shape
    cdim = w_dw.shape[2]
    x2 = x.reshape(b * t, d)
    if p is None and t % FUSED_TIME_TILE == 0 and w_up.dtype == F32:
        p, a_out, new_conv, w_out, w_up, w_down = _in_proj_conv(
            x, g_mix, w_in, conv_state, w_dw, b_dw, ln_g, ln_b, w_out, w_up, w_down, layer, FUSED_TIME_TILE)
        p_hgrn, hgrn_col0 = p.reshape(p.shape[0], b, t, p.shape[2]), None
    else:
        w_out, w_up, w_down = w_out.astype(BF16), w_up.astype(BF16), w_down.astype(BF16)
        if p is None:
            p = _in_proj(x2, g_mix, w_in)
        p_hgrn, hgrn_col0 = p.reshape(b, t, p.shape[1]), 2 * cdim
        a_out, new_conv = _conv_group(p_hgrn, conv_state, w_dw, b_dw, ln_g, ln_b, layer)
    b_out, new_s = _hgrn_group(p_hgrn, s0, lb_logits, hgrn_g, hgrn_col0, layer)
    h = _out_proj(x2, a_out.reshape(b * t, -1), b_out.reshape(b * t, -1), w_out)
    y = _mlp(h, g_mlp, w_up, w_down, g_final)
    return y.reshape(b, t, d), new_conv, new_s, w_out, w_up, w_down


def kernel(x_prompt, x_sample, state_conv, state_hgrn, norm_mix_g, w_in, w_dw, b_dw, ln_conv_g, ln_conv_b,
           lb_logits, hgrn_norm_g, w_out, norm_mlp_g, w_up, w_down, norm_final_g):
    depth = w_in.shape[0]
    assert depth == 1, "single-layer trunk"
    l = 0
    bp = x_prompt.shape[0]
    g_mix = norm_mix_g[l][None]
    n_sample = x_sample.shape[0] * x_sample.shape[1]
    if n_sample <= SINGLE_ROW_TILE:
        p_sample, w_in_b = _in_proj_cast(x_sample.reshape(n_sample, -1), g_mix, w_in[l])
    else:
        p_sample, w_in_b = None, w_in[l].astype(BF16)
    mix_w = (g_mix, w_in_b, w_dw, b_dw[l][None], ln_conv_g[l][None], ln_conv_b[l][None], hgrn_norm_g[l][None])
    g_mlp, g_final = norm_mlp_g[l][None], norm_final_g[None]
    zero_conv = jnp.zeros((depth, bp) + state_conv.shape[2:], state_conv.dtype)
    zero_s = jnp.zeros((bp,) + state_hgrn.shape[2:], state_hgrn.dtype)
    yp, cp, sp, w_out_b, w_up_b, w_down_b = _layer(x_prompt, zero_conv, zero_s, lb_logits, l, *mix_w,
                                                   w_out[l], g_mlp, w_up[l], w_down[l], g_final)
    ys, cs, ss, _, _, _ = _layer(x_sample, state_conv, state_hgrn[l], lb_logits, l, *mix_w,
                                 w_out_b, g_mlp, w_up_b, w_down_b, g_final, p=p_sample)
    return (yp, ys, cp, sp[None], cs, ss[None])
```
